```python
import math
import jax, jax.numpy as jnp
from jax import lax
import numpy as np

D_MODEL = 1024
BATCH = 8
SEQ = 4096
DEPTH = 2
DEC_BATCH = 32
DEC_SEQ = 1
PAST_LEN = 16384
PAGE_SIZE = 128

N_HEADS = D_MODEL // 128
HEAD_DIM = 64
ATTN_WIDTH = N_HEADS * 2 * HEAD_DIM
D_CONV = D_MODEL // 2
CONV_WIDTH = 31
D_FF = 4 * D_MODEL
Q_BLOCK = 128
OFF_Q = 2 * D_CONV
OFF_K = OFF_Q + ATTN_WIDTH
OFF_V = OFF_K + ATTN_WIDTH
OFF_G = OFF_V + ATTN_WIDTH
IN_WIDTH = OFF_G + 2 * D_MODEL
NORM_EPS = 1e-6

kernel_name = "hybrid_conformer_conv_diff_attn_decode_step"


def rmsnorm(x, g):
    xf = x.astype(jnp.float32)
    y = xf * lax.rsqrt(jnp.mean(jnp.square(xf), axis=-1, keepdims=True) + NORM_EPS)
    return (y * g.astype(jnp.float32)).astype(x.dtype)


def layernorm(x, g, b):
    xf = x.astype(jnp.float32)
    mu = jnp.mean(xf, axis=-1, keepdims=True)
    var = jnp.mean(jnp.square(xf - mu), axis=-1, keepdims=True)
    y = (xf - mu) * lax.rsqrt(var + 1e-5) * g.astype(jnp.float32) + b.astype(jnp.float32)
    return y.astype(x.dtype)


def depthwise_valid(u_pad, w_dw, b_dw):
    out = lax.conv_general_dilated(
        u_pad, w_dw[:, None, :], window_strides=(1,), padding='VALID',
        dimension_numbers=('NWC', 'WIO', 'NWC'), feature_group_count=u_pad.shape[-1])
    return out + b_dw


def prompt_diff_attention(q, k, v, lam):
    B, S = q.shape[0], q.shape[1]
    nb = S // Q_BLOCK
    k4 = k.reshape(B, S, N_HEADS, 2, HEAD_DIM)
    qb = q.reshape(B, nb, Q_BLOCK, N_HEADS, 2, HEAD_DIM).transpose(1, 0, 2, 3, 4, 5)
    kpos = jnp.arange(S)
    scale = 1.0 / math.sqrt(HEAD_DIM)

    def one_block(args):
        i, qi = args
        s = jnp.einsum('bqhmd,bkhmd->bhmqk', qi, k4).astype(jnp.float32) * scale
        qpos = i * Q_BLOCK + jnp.arange(Q_BLOCK)
        mask = kpos[None, :] <= qpos[:, None]
        p = jax.nn.softmax(jnp.where(mask, s, -jnp.inf), axis=-1)
        a = p[:, :, 0] - lam * p[:, :, 1]
        return jnp.einsum('bhqk,bkhe->bqhe', a.astype(v.dtype), v)

    out = lax.map(one_block, (jnp.arange(nb), qb))
    return out.transpose(1, 0, 2, 3, 4).reshape(B, S, N_HEADS, 2 * HEAD_DIM)


def sample_diff_attention(q, k_new, v_new, ck, cv, page_table, lam):
    T = q.shape[1]
    past = page_table.shape[1] * PAGE_SIZE
    kpos = jnp.arange(past + T)
    qpos = past + jnp.arange(T)
    mask = kpos[None, :] <= qpos[:, None]
    scale = 1.0 / math.sqrt(HEAD_DIM)

    def one_seq(args):
        qi, kn, vn, pt = args
        kp = ck[pt].reshape(past, N_HEADS, 2 * HEAD_DIM)
        vp = cv[pt].reshape(past, N_HEADS, 2 * HEAD_DIM)
        ka = jnp.concatenate([kp, kn.astype(kp.dtype)], axis=0).reshape(past + T, N_HEADS, 2, HEAD_DIM)
        va = jnp.concatenate([vp, vn.astype(vp.dtype)], axis=0)
        s = jnp.einsum('qhmd,khmd->hmqk', qi, ka).astype(jnp.float32) * scale
        p = jax.nn.softmax(jnp.where(mask, s, -jnp.inf), axis=-1)
        a = p[:, 0] - lam * p[:, 1]
        return jnp.einsum('hqk,khe->qhe', a.astype(va.dtype), va)

    return lax.map(one_seq, (q, k_new, v_new, page_table))


def setup_inputs(seed: int = 0) -> dict:
    key = jax.random.key(seed)
    ks = jax.random.split(key, 24)
    n_pages = PAST_LEN // PAGE_SIZE
    n_pool = (DEC_BATCH * n_pages * 5) // 4
    f32 = jnp.float32
    nrm = lambda k, shape, s: jax.random.normal(k, shape, f32) * s
    x_prompt = nrm(ks[0], (BATCH, SEQ, D_MODEL), 1.0)
    x_sample = nrm(ks[1], (DEC_BATCH, DEC_SEQ, D_MODEL), 1.0)
    cache_k = nrm(ks[2], (DEPTH, n_pool, PAGE_SIZE, N_HEADS, 2 * HEAD_DIM), 1.0)
    cache_v = nrm(ks[3], (DEPTH, n_pool, PAGE_SIZE, N_HEADS, 2 * HEAD_DIM), 1.0)
    state_conv = nrm(ks[4], (DEPTH, DEC_BATCH, CONV_WIDTH - 1, D_CONV), 0.5)
    perm = jax.random.permutation(ks[5], n_pool)
    page_table = perm[:DEC_BATCH * n_pages].reshape(DEC_BATCH, n_pages).astype(jnp.int32)
    gain = lambda k, n: 1.0 + nrm(k, (DEPTH, n), 0.05)
    return {
        "x_prompt": x_prompt,
        "x_sample": x_sample,
        "cache_k": cache_k,
        "cache_v": cache_v,
        "state_conv": state_conv,
        "page_table": page_table,
        "norm_mix_pre": gain(ks[6], D_MODEL),
        "norm_mix_post": gain(ks[7], D_MODEL),
        "norm_mlp_pre": gain(ks[8], D_MODEL),
        "norm_mlp_post": gain(ks[9], D_MODEL),
        "w_in": nrm(ks[10], (DEPTH, D_MODEL, IN_WIDTH), D_MODEL ** -0.5),
        "w_dw": nrm(ks[11], (DEPTH, CONV_WIDTH, D_CONV), CONV_WIDTH ** -0.5),
        "b_dw": nrm(ks[12], (DEPTH, D_CONV), 0.01),
        "conv_ln_g": gain(ks[13], D_CONV),
        "conv_ln_b": nrm(ks[14], (DEPTH, D_CONV), 0.01),
        "w_conv_out": nrm(ks[15], (DEPTH, D_CONV, D_MODEL), D_CONV ** -0.5),
        "lambda_qk": nrm(ks[16], (DEPTH, 4, HEAD_DIM), 0.1),
        "subln_g": gain(ks[17], 2 * HEAD_DIM),
        "w_attn_out": nrm(ks[18], (DEPTH, ATTN_WIDTH, D_MODEL), ATTN_WIDTH ** -0.5),
        "w_out": nrm(ks[19], (DEPTH, D_MODEL, D_MODEL), D_MODEL ** -0.5),
        "w_up": nrm(ks[20], (DEPTH, D_MODEL, D_FF), D_MODEL ** -0.5),
        "w_down": nrm(ks[21], (DEPTH, D_FF, D_MODEL), D_FF ** -0.5),
    }


def reference(x_prompt, x_sample, cache_k, cache_v, state_conv, page_table,
              norm_mix_pre, norm_mix_post, norm_mlp_pre, norm_mlp_post,
              w_in, w_dw, b_dw, conv_ln_g, conv_ln_b, w_conv_out,
              lambda_qk, subln_g, w_attn_out, w_out, w_up, w_down):

    def trunk_layer(x, l, attn_fn, conv_fn):
        B, T = x.shape[0], x.shape[1]
        h = rmsnorm(x, norm_mix_pre[l])
        z = h @ w_in[l]
        a_glu, b_glu = z[..., :D_CONV], z[..., D_CONV:OFF_Q]
        u = a_glu * jax.nn.sigmoid(b_glu)
        uc, conv_state = conv_fn(u, l)
        y_conv = jax.nn.silu(layernorm(uc, conv_ln_g[l], conv_ln_b[l])) @ w_conv_out[l]
        q = z[..., OFF_Q:OFF_K].reshape(B, T, N_HEADS, 2, HEAD_DIM)
        k = z[..., OFF_K:OFF_V].reshape(B, T, N_HEADS, 2 * HEAD_DIM)
        v = z[..., OFF_V:OFF_G].reshape(B, T, N_HEADS, 2 * HEAD_DIM)
        lam_init = 0.8 - 0.6 * math.exp(-0.3 * l)
        lq = lambda_qk[l].astype(jnp.float32)
        lam = jnp.exp(jnp.sum(lq[0] * lq[1])) - jnp.exp(jnp.sum(lq[2] * lq[3])) + lam_init
        o = attn_fn(q, k, v, lam, l)
        o = rmsnorm(o, subln_g[l]) * (1.0 - lam_init)
        y_attn = o.reshape(B, T, ATTN_WIDTH) @ w_attn_out[l]
        g = jax.nn.sigmoid(z[..., OFF_G:])
        m = g[..., :D_MODEL] * y_conv + g[..., D_MODEL:] * y_attn
        x = x + rmsnorm(m @ w_out[l], norm_mix_post[l])
        h2 = rmsnorm(x, norm_mlp_pre[l])
        f = jnp.square(jax.nn.relu(h2 @ w_up[l])) @ w_down[l]
        x = x + rmsnorm(f, norm_mlp_post[l])
        return x, k, v, conv_state

    def conv_prompt_fn(u, l):
        u_pad = jnp.pad(u, ((0, 0), (CONV_WIDTH - 1, 0), (0, 0)))
        return depthwise_valid(u_pad, w_dw[l], b_dw[l]), u_pad[:, -(CONV_WIDTH - 1):]

    def attn_prompt_fn(q, k, v, lam, l):
        return prompt_diff_attention(q, k, v, lam)

    def conv_sample_fn(u, l):
        buf = jnp.concatenate([state_conv[l].astype(u.dtype), u], axis=1)
        return depthwise_valid(buf, w_dw[l], b_dw[l]), buf[:, u.shape[1]:]

    def attn_sample_fn(q, k, v, lam, l):
        return sample_diff_attention(q, k, v, cache_k[l], cache_v[l], page_table, lam)

    xp = x_prompt
    xs = x_sample
    kp_list, vp_list, cp_list = [], [], []
    ks_list, vs_list, cs_list = [], [], []
    for l in range(DEPTH):
        xp, kp, vp, cp = trunk_layer(xp, l, attn_prompt_fn, conv_prompt_fn)
        kp_list.append(kp); vp_list.append(vp); cp_list.append(cp)
        xs, k_s, v_s, c_s = trunk_layer(xs, l, attn_sample_fn, conv_sample_fn)
        ks_list.append(k_s); vs_list.append(v_s); cs_list.append(c_s)

    k_prompt = jnp.stack(kp_list, axis=0)
    v_prompt = jnp.stack(vp_list, axis=0)
    conv_prompt = jnp.stack(cp_list, axis=0)
    k_sample = jnp.stack(ks_list, axis=0)
    v_sample = jnp.stack(vs_list, axis=0)
    conv_sample = jnp.stack(cs_list, axis=0)
    return (xp, xs, k_prompt, v_prompt, conv_prompt, k_sample, v_sample, conv_sample)
```

```python
import functools
import math

import jax
import jax.numpy as jnp
from jax import lax
from jax.experimental import pallas as pl
from jax.experimental.pallas import tpu as pltpu

F32 = jnp.float32
BF16 = jnp.bfloat16

HEAD_DIM = 64
HEAD_WIDTH = 2 * HEAD_DIM
CONV_WIDTH = 31
HALO_ROWS = 32
NORM_EPS = 1e-6
LN_EPS = 1e-5
NEG_BIG = -1e30
VMEM_LIMIT = 60 * 1024 * 1024

TM_IN = 512
TM_POST = 256
TQ = 256
DEC_PAGES = 8
CONV_ROWS = 64


def _dot(a, b):
    return jnp.dot(a, b, preferred_element_type=F32)


def _dot_nt(a, b):
    return lax.dot_general(a, b, (((1,), (1,)), ((), ())), preferred_element_type=F32)


def _rms(x, g):
    y = x * lax.rsqrt(jnp.mean(jnp.square(x), axis=-1, keepdims=True) + NORM_EPS)
    return y * g


def _lam(lq, lam_init):
    a = jnp.sum(lq[0:1] * lq[1:2], axis=-1, keepdims=True)
    b = jnp.sum(lq[2:3] * lq[3:4], axis=-1, keepdims=True)
    return jnp.exp(a) - jnp.exp(b) + lam_init


def _const_spec(shape):
    n = len(shape)
    return pl.BlockSpec(shape, lambda *_: (0,) * n, pipeline_mode=pl.Buffered(1))


def _inproj_kernel(x_ref, g_ref, w_ref, u_ref, q_ref, k_ref, v_ref, kb_ref, vb_ref,
                   gate_ref, *, d_conv, d_model, attn_w):
    h = _rms(x_ref[...], g_ref[...]).astype(BF16)
    off_q = 2 * d_conv
    off_k = off_q + attn_w
    off_v = off_k + attn_w
    off_g = off_v + attn_w
    zab = _dot(h, w_ref[:, 0:off_q])
    u_ref[...] = zab[:, :d_conv] * jax.nn.sigmoid(zab[:, d_conv:])
    q = _dot(h, w_ref[:, off_q:off_k])
    q_ref[...] = (q * (1.0 / math.sqrt(HEAD_DIM))).astype(BF16)
    k = _dot(h, w_ref[:, off_k:off_v])
    k_ref[...] = k
    kb_ref[...] = k.astype(BF16)
    v = _dot(h, w_ref[:, off_v:off_g])
    v_ref[...] = v
    vb_ref[...] = v.astype(BF16)
    gate_ref[...] = jax.nn.sigmoid(_dot(h, w_ref[:, off_g:])).astype(BF16)


def _inproj(x, g, w, *, tm, d_conv, attn_w):
    t, d_model = x.shape
    in_w = w.shape[1]
    row = lambda n: pl.BlockSpec((tm, n), lambda i: (i, 0))
    kern = functools.partial(_inproj_kernel, d_conv=d_conv, d_model=d_model, attn_w=attn_w)
    return pl.pallas_call(
        kern,
        grid=(t // tm,),
        in_specs=[row(d_model), _const_spec((1, d_model)), _const_spec((d_model, in_w))],
        out_specs=[row(d_conv), row(attn_w), row(attn_w), row(attn_w), row(attn_w),
                   row(attn_w), row(2 * d_model)],
        out_shape=[
            jax.ShapeDtypeStruct((t, d_conv), F32),
            jax.ShapeDtypeStruct((t, attn_w), BF16),
            jax.ShapeDtypeStruct((t, attn_w), F32),
            jax.ShapeDtypeStruct((t, attn_w), F32),
            jax.ShapeDtypeStruct((t, attn_w), BF16),
            jax.ShapeDtypeStruct((t, attn_w), BF16),
            jax.ShapeDtypeStruct((t, 2 * d_model), BF16),
        ],
        compiler_params=pltpu.CompilerParams(
            dimension_semantics=("arbitrary",), vmem_limit_bytes=VMEM_LIMIT),
    )(x, g, w)


def _attn_prompt_kernel(lq_ref, sg_ref, q_ref, k_ref, v_ref, o_ref, *, tq, lam_init):
    qi = pl.program_id(2)
    q = q_ref[0]
    lane = lax.broadcasted_iota(jnp.int32, q.shape, 1)
    zero = jnp.zeros_like(q)
    qs = jnp.concatenate([jnp.where(lane < HEAD_DIM, q, zero),
                          jnp.where(lane >= HEAD_DIM, q, zero)], axis=0)

    def step(j, carry, masked):
        m, l, acc = carry
        start = pl.multiple_of(j * tq, tq)
        kb = k_ref[0, pl.ds(start, tq), :]
        vb = v_ref[0, pl.ds(start, tq), :]
        s = _dot_nt(qs, kb)
        if masked:
            r = lax.broadcasted_iota(jnp.int32, s.shape, 0)
            c = lax.broadcasted_iota(jnp.int32, s.shape, 1)
            s = jnp.where(c <= jnp.where(r >= tq, r - tq, r), s, NEG_BIG)
        m_new = jnp.maximum(m, jnp.max(s, axis=-1, keepdims=True))
        alpha = jnp.exp(m - m_new)
        p = jnp.exp(s - m_new)
        l = alpha * l + jnp.sum(p, axis=-1, keepdims=True)
        acc = alpha * acc + _dot(p.astype(BF16), vb)
        return m_new, l, acc

    init = (jnp.full((2 * tq, 1), NEG_BIG, F32), jnp.zeros((2 * tq, 1), F32),
            jnp.zeros((2 * tq, HEAD_WIDTH), F32))
    carry = lax.fori_loop(0, qi, lambda j, c: step(j, c, False), init)
    _, l, acc = step(qi, carry, True)

    lam = _lam(lq_ref[...], lam_init)
    o = acc[:tq] / l[:tq] - lam * (acc[tq:] / l[tq:])
    o = _rms(o, sg_ref[...]) * (1.0 - lam_init)
    o_ref[0] = o.astype(o_ref.dtype)


def _attn_prompt(lq, sg, q, kb, vb, *, n_heads, tq, lam_init):
    b, s, _ = q.shape
    kern = functools.partial(_attn_prompt_kernel, tq=tq, lam_init=lam_init)
    return pl.pallas_call(
        kern,
        grid=(b, n_heads, s // tq),
        in_specs=[
            pl.BlockSpec(lq.shape, lambda bi, h, i: (0, 0)),
            pl.BlockSpec(sg.shape, lambda bi, h, i: (0, 0)),
            pl.BlockSpec((1, tq, HEAD_WIDTH), lambda bi, h, i: (bi, i, h)),
            pl.BlockSpec((1, s, HEAD_WIDTH), lambda bi, h, i: (bi, 0, h)),
            pl.BlockSpec((1, s, HEAD_WIDTH), lambda bi, h, i: (bi, 0, h)),
        ],
        out_specs=pl.BlockSpec((1, tq, HEAD_WIDTH), lambda bi, h, i: (bi, i, h)),
        out_shape=jax.ShapeDtypeStruct(q.shape, BF16),
        compiler_params=pltpu.CompilerParams(
            dimension_semantics=("arbitrary", "arbitrary", "arbitrary"),
            vmem_limit_bytes=VMEM_LIMIT),
    )(lq, sg, q, kb, vb)


def _attn_decode_kernel(pt_ref, lq_ref, sg_ref, q_ref, kn_ref, vn_ref, *rest,
                        n_heads, n_pages_step, lam_init):
    del pt_ref
    k_refs = rest[:n_pages_step]
    v_refs = rest[n_pages_step:2 * n_pages_step]
    o_ref, m_sc, l_sc, acc_sc = rest[2 * n_pages_step:]
    j = pl.program_id(1)
    rows = 2 * n_heads
    width = n_heads * HEAD_WIDTH

    q = jnp.broadcast_to(q_ref[0].astype(F32), (rows, width))
    r = lax.broadcasted_iota(jnp.int32, (rows, width), 0)
    c = lax.broadcasted_iota(jnp.int32, (rows, width), 1)
    sel = ((c // HEAD_WIDTH) == (r % n_heads)) & (((c % HEAD_WIDTH) // HEAD_DIM) == (r // n_heads))
    qrows_f = jnp.where(sel, q, 0.0)
    qrows = qrows_f.astype(BF16)

    @pl.when(j == 0)
    def _():
        m_sc[...] = jnp.sum(qrows_f * kn_ref[0], axis=-1, keepdims=True)
        l_sc[...] = jnp.ones_like(l_sc)
        acc_sc[...] = jnp.broadcast_to(vn_ref[0], acc_sc.shape)

    s = jnp.concatenate([_dot_nt(qrows, kr[...].astype(BF16)) for kr in k_refs], axis=1)
    m = m_sc[...]
    m_new = jnp.maximum(m, jnp.max(s, axis=-1, keepdims=True))
    alpha = jnp.exp(m - m_new)
    p = jnp.exp(s - m_new)
    l_sc[...] = alpha * l_sc[...] + jnp.sum(p, axis=-1, keepdims=True)
    m_sc[...] = m_new
    pb = p.astype(BF16)
    pv = _dot(pb[:, 0:128], v_refs[0][...].astype(BF16))
    for i in range(1, n_pages_step):
        pv = pv + _dot(pb[:, i * 128:(i + 1) * 128], v_refs[i][...].astype(BF16))
    acc_sc[...] = alpha * acc_sc[...] + pv

    @pl.when(j == pl.num_programs(1) - 1)
    def _():
        lam = _lam(lq_ref[...], lam_init)
        a = acc_sc[...] / l_sc[...]
        d = a[:n_heads] - lam * a[n_heads:]
        rr = lax.broadcasted_iota(jnp.int32, d.shape, 0)
        cc = lax.broadcasted_iota(jnp.int32, d.shape, 1)
        d = jnp.where((cc // HEAD_WIDTH) == rr, d, 0.0)
        ms = jnp.sum(jnp.square(d), axis=-1, keepdims=True) * (1.0 / HEAD_WIDTH)
        d = d * lax.rsqrt(ms + NORM_EPS)
        o = jnp.sum(d, axis=0, keepdims=True) * sg_ref[...] * (1.0 - lam_init)
        o_ref[0] = o.astype(o_ref.dtype)


def _attn_decode(page_table, lq, sg_tiled, q, kn, vn, cache_k, cache_v, *, layer,
                 n_heads, pages_step, lam_init):
    db, width = q.shape
    n_pages = page_table.shape[1]
    page_size = cache_k.shape[2]
    rows = 2 * n_heads
    q3, kn3, vn3 = (a.reshape(db, 1, width) for a in (q, kn, vn))
    tok = pl.BlockSpec((1, 1, width), lambda b, j, pt: (b, 0, 0))

    def page_spec(i):
        return pl.BlockSpec((None, None, page_size, width),
                            lambda b, j, pt: (layer, pt[b, j * pages_step + i], 0, 0))

    kern = functools.partial(_attn_decode_kernel, n_heads=n_heads,
                             n_pages_step=pages_step, lam_init=lam_init)
    out = pl.pallas_call(
        kern,
        grid_spec=pltpu.PrefetchScalarGridSpec(
            num_scalar_prefetch=1,
            grid=(db, n_pages // pages_step),
            in_specs=[pl.BlockSpec(lq.shape, lambda b, j, pt: (0, 0)),
                      pl.BlockSpec(sg_tiled.shape, lambda b, j, pt: (0, 0)),
                      tok, tok, tok]
                     + [page_spec(i) for i in range(pages_step)]
                     + [page_spec(i) for i in range(pages_step)],
            out_specs=pl.BlockSpec((1, 1, width), lambda b, j, pt: (b, 0, 0)),
            scratch_shapes=[pltpu.VMEM((rows, 1), F32), pltpu.VMEM((rows, 1), F32),
                            pltpu.VMEM((rows, width), F32)],
        ),
        out_shape=jax.ShapeDtypeStruct((db, 1, width), BF16),
        compiler_params=pltpu.CompilerParams(
            dimension_semantics=("arbitrary", "arbitrary"), vmem_limit_bytes=VMEM_LIMIT),
    )(page_table, lq, sg_tiled, q3, kn3, vn3,
      *([cache_k] * pages_step), *([cache_v] * pages_step))
    return out.reshape(db, width)


def _mix_and_mlp(x, uc, o, gate, lng, lnb, wco, wao, wo, wup, wdn, n_post, n_pre, n_mpost,
                 *, d_model, ff_chunk):
    mu = jnp.mean(uc, axis=-1, keepdims=True)
    dc = uc - mu
    var = jnp.mean(jnp.square(dc), axis=-1, keepdims=True)
    ln = dc * lax.rsqrt(var + LN_EPS) * lng[...] + lnb[...]
    y_conv = _dot((ln * jax.nn.sigmoid(ln)).astype(BF16), wco[...])
    y_attn = _dot(o, wao[...])
    mix = gate[:, :d_model].astype(F32) * y_conv + gate[:, d_model:].astype(F32) * y_attn
    x1 = x + _rms(_dot(mix.astype(BF16), wo[...]), n_post[...])
    h2 = _rms(x1, n_pre[...]).astype(BF16)
    d_ff = wup.shape[1]
    f = None
    for c0 in range(0, d_ff, ff_chunk):
        a = jnp.square(jnp.maximum(_dot(h2, wup[:, c0:c0 + ff_chunk]), 0.0)).astype(BF16)
        part = _dot(a, wdn[c0:c0 + ff_chunk, :])
        f = part if f is None else f + part
    return x1 + _rms(f, n_mpost[...])


def _post_prompt_kernel(x_ref, u_ref, halo_ref, o_ref, gate_ref, wdw_ref, bdw_ref, lng, lnb,
                        wco, wao, wo, wup, wdn, n_post, n_pre, n_mpost, out_ref, ext_sc, uc_sc,
                        *, tm, tiles_per_seq, d_model, ff_chunk):
    i = pl.program_id(0)
    first = (i % tiles_per_seq) == 0
    halo = halo_ref[...]
    ext_sc[0:HALO_ROWS, :] = jnp.where(first, jnp.zeros_like(halo), halo)
    ext_sc[HALO_ROWS:, :] = u_ref[...]
    lead = HALO_ROWS - (CONV_WIDTH - 1)
    for r0 in range(0, tm, CONV_ROWS):
        acc = jnp.broadcast_to(bdw_ref[...], (CONV_ROWS, u_ref.shape[1]))
        for j in range(CONV_WIDTH):
            acc = acc + ext_sc[r0 + lead + j:r0 + lead + j + CONV_ROWS, :] * wdw_ref[j:j + 1, :]
        uc_sc[r0:r0 + CONV_ROWS, :] = acc
    out_ref[...] = _mix_and_mlp(x_ref[...], uc_sc[...], o_ref[...], gate_ref[...], lng, lnb,
                                wco, wao, wo, wup, wdn, n_post, n_pre, n_mpost,
                                d_model=d_model, ff_chunk=ff_chunk)


def _post_sample_kernel(x_ref, u_ref, st_ref, o_ref, gate_ref, wdw_ref, bdw_ref, lng, lnb,
                        wco, wao, wo, wup, wdn, n_post, n_pre, n_mpost, out_ref,
                        *, d_model, ff_chunk):
    uc = bdw_ref[...] + u_ref[...] * wdw_ref[CONV_WIDTH - 1:CONV_WIDTH, :]
    for j in range(CONV_WIDTH - 1):
        uc = uc + st_ref[j] * wdw_ref[j:j + 1, :]
    out_ref[...] = _mix_and_mlp(x_ref[...], uc, o_ref[...], gate_ref[...], lng, lnb,
                                wco, wao, wo, wup, wdn, n_post, n_pre, n_mpost,
                                d_model=d_model, ff_chunk=ff_chunk)


def _weight_specs(ws):
    return [_const_spec(w.shape) for w in ws]


def _post_prompt(x, u, o, gate, weights, *, tm, seq, ff_chunk):
    t, d_model = x.shape
    d_conv = u.shape[1]
    row = lambda n: pl.BlockSpec((tm, n), lambda i: (i, 0))
    halo_per_tile = tm // HALO_ROWS
    halo_spec = pl.BlockSpec((HALO_ROWS, d_conv),
                             lambda i: (jnp.maximum(i * halo_per_tile - 1, 0), 0))
    kern = functools.partial(_post_prompt_kernel, tm=tm, tiles_per_seq=seq // tm,
                             d_model=d_model, ff_chunk=ff_chunk)
    return pl.pallas_call(
        kern,
        grid=(t // tm,),
        in_specs=[row(d_model), row(d_conv), halo_spec, row(o.shape[1]), row(2 * d_model)]
                 + _weight_specs(weights),
        out_specs=row(d_model),
        out_shape=jax.ShapeDtypeStruct((t, d_model), F32),
        scratch_shapes=[pltpu.VMEM((tm + HALO_ROWS, d_conv), F32),
                        pltpu.VMEM((tm, d_conv), F32)],
        compiler_params=pltpu.CompilerParams(
            dimension_semantics=("arbitrary",), vmem_limit_bytes=VMEM_LIMIT),
    )(x, u, u, o, gate, *weights)


def _post_sample(x, u, state_t, o, gate, weights, *, ff_chunk):
    t, d_model = x.shape
    full = lambda a: pl.BlockSpec(a.shape, lambda i: (0,) * a.ndim)
    kern = functools.partial(_post_sample_kernel, d_model=d_model, ff_chunk=ff_chunk)
    return pl.pallas_call(
        kern,
        grid=(1,),
        in_specs=[full(x), full(u), full(state_t), full(o), full(gate)] + _weight_specs(weights),
        out_specs=pl.BlockSpec((t, d_model), lambda i: (0, 0)),
        out_shape=jax.ShapeDtypeStruct((t, d_model), F32),
        compiler_params=pltpu.CompilerParams(
            dimension_semantics=("arbitrary",), vmem_limit_bytes=VMEM_LIMIT),
    )(x, u, state_t, o, gate, *weights)


def kernel(x_prompt, x_sample, cache_k, cache_v, state_conv, page_table, norm_mix_pre, norm_mix_post, norm_mlp_pre, norm_mlp_post, w_in, w_dw, b_dw, conv_ln_g, conv_ln_b, w_conv_out, lambda_qk, subln_g, w_attn_out, w_out, w_up, w_down):
    batch, seq, d_model = x_prompt.shape
    dec_batch, dec_seq, _ = x_sample.shape
    assert dec_seq == 1
    depth, n_pool, page_size, n_heads, head_w = cache_k.shape
    assert head_w == HEAD_WIDTH and state_conv.shape[2] == CONV_WIDTH - 1
    d_conv = state_conv.shape[3]
    attn_w = n_heads * HEAD_WIDTH
    ff_chunk = d_model

    ck = cache_k.reshape(depth, n_pool, page_size, attn_w)
    cv = cache_v.reshape(depth, n_pool, page_size, attn_w)
    xp = x_prompt.reshape(batch * seq, d_model)
    xs = x_sample.reshape(dec_batch, d_model)
    row = lambda a: a.reshape(1, -1)

    kp, vp, cp, ks, vs, cs = [], [], [], [], [], []
    for l in range(depth):
        lam_init = 0.8 - 0.6 * math.exp(-0.3 * l)
        w_in_b = w_in[l].astype(BF16)
        weights = (w_dw[l], row(b_dw[l]), row(conv_ln_g[l]), row(conv_ln_b[l]),
                   w_conv_out[l].astype(BF16), w_attn_out[l].astype(BF16),
                   w_out[l].astype(BF16), w_up[l].astype(BF16), w_down[l].astype(BF16),
                   row(norm_mix_post[l]), row(norm_mlp_pre[l]), row(norm_mlp_post[l]))
        g_pre = row(norm_mix_pre[l])
        lq = lambda_qk[l]
        sg = row(subln_g[l])

        u, q, k, v, kb, vb, gate = _inproj(xp, g_pre, w_in_b, tm=TM_IN, d_conv=d_conv,
                                           attn_w=attn_w)
        o = _attn_prompt(lq, sg, q.reshape(batch, seq, attn_w), kb.reshape(batch, seq, attn_w),
                         vb.reshape(batch, seq, attn_w), n_heads=n_heads, tq=TQ,
                         lam_init=lam_init)
        xp = _post_prompt(xp, u, o.reshape(batch * seq, attn_w), gate, weights, tm=TM_POST,
                          seq=seq, ff_chunk=ff_chunk)
        kp.append(k.reshape(batch, seq, n_heads, HEAD_WIDTH))
        vp.append(v.reshape(batch, seq, n_heads, HEAD_WIDTH))
        cp.append(u.reshape(batch, seq, d_conv)[:, seq - (CONV_WIDTH - 1):])

        u, q, k, v, _, _, gate = _inproj(xs, g_pre, w_in_b, tm=dec_batch, d_conv=d_conv,
                                         attn_w=attn_w)
        o = _attn_decode(page_table, lq, jnp.tile(sg, (1, n_heads)), q, k, v, ck, cv, layer=l,
                         n_heads=n_heads, pages_step=DEC_PAGES, lam_init=lam_init)
        st = state_conv[l]
        xs = _post_sample(xs, u, st.transpose(1, 0, 2), o, gate, weights, ff_chunk=ff_chunk)
        ks.append(k.reshape(dec_batch, 1, n_heads, HEAD_WIDTH))
        vs.append(v.reshape(dec_batch, 1, n_heads, HEAD_WIDTH))
        cs.append(jnp.concatenate([st[:, 1:], u[:, None, :]], axis=1))

    return (xp.reshape(batch, seq, d_model), xs.reshape(dec_batch, 1, d_model),
            jnp.stack(kp), jnp.stack(vp), jnp.stack(cp),
            jnp.stack(ks), jnp.stack(vs), jnp.stack(cs))
```

```python
import functools
import math

import jax
import jax.numpy as jnp
from jax import lax
from jax.experimental import pallas as pl
from jax.experimental.pallas import tpu as pltpu

F32 = jnp.float32
BF16 = jnp.bfloat16

HEAD_DIM = 64
HEAD_WIDTH = 2 * HEAD_DIM
CONV_WIDTH = 31
SUBLANES = 8
HALO_ROWS = 32
NORM_EPS = 1e-6
LN_EPS = 1e-5
NEG_BIG = -1e30
Q_SCALE = math.log2(math.e) / math.sqrt(HEAD_DIM)
VMEM_LIMIT = 60 * 1024 * 1024

TM_IN = 512
TM_POST = 512
TQ = 512
DEC_PAGES = 8
CONV_ROWS = 64
CONV_LANES = 128
PROJ_COLS = 512


def _dot(a, b):
    return jnp.dot(a, b, preferred_element_type=F32)


def _dot_nt(a, b):
    return lax.dot_general(a, b, (((1,), (1,)), ((), ())), preferred_element_type=F32)


def _rms(x, g):
    y = x * lax.rsqrt(jnp.mean(jnp.square(x), axis=-1, keepdims=True) + NORM_EPS)
    return y * g


def _lam(lq, lam_init):
    a = jnp.sum(lq[0:1] * lq[1:2], axis=-1, keepdims=True)
    b = jnp.sum(lq[2:3] * lq[3:4], axis=-1, keepdims=True)
    return jnp.exp(a) - jnp.exp(b) + lam_init


def _const_spec(shape):
    n = len(shape)
    return pl.BlockSpec(shape, lambda *_: (0,) * n, pipeline_mode=pl.Buffered(1))


def _params(n_grid):
    return pltpu.CompilerParams(dimension_semantics=("arbitrary",) * n_grid,
                                vmem_limit_bytes=VMEM_LIMIT)


def _glu(h, w_ref, d_conv):
    zab = _dot(h, w_ref[:, 0:2 * d_conv])
    return zab[:, :d_conv] * jax.nn.sigmoid(zab[:, d_conv:])


def _store_q(q_ref, cs, z):
    q_ref[:, cs] = (z * Q_SCALE).astype(BF16)


def _store_kv(f_ref, b_ref, cs, z):
    f_ref[:, cs] = z
    if b_ref is not None:
        b_ref[:, cs] = z.astype(BF16)


def _store_gate(gate_ref, cs, z):
    gate_ref[:, cs] = jax.nn.sigmoid(z).astype(BF16)


def _qkvg_plan(w_ref, q_ref, k_ref, v_ref, kb_ref, vb_ref, gate_ref, *, d_conv, attn_w):
    off_q = 2 * d_conv
    off_k = off_q + attn_w
    off_v = off_k + attn_w
    off_g = off_v + attn_w
    return [(off_q, attn_w // PROJ_COLS, functools.partial(_store_q, q_ref)),
            (off_k, attn_w // PROJ_COLS, functools.partial(_store_kv, k_ref, kb_ref)),
            (off_v, attn_w // PROJ_COLS, functools.partial(_store_kv, v_ref, vb_ref)),
            (off_g, (w_ref.shape[1] - off_g) // PROJ_COLS,
             functools.partial(_store_gate, gate_ref))]


def _project_piece(h_ref, w_ref, off, i, store):
    c0 = i * PROJ_COLS
    w0 = off + c0
    if not isinstance(i, int):
        c0 = pl.multiple_of(c0, PROJ_COLS)
        w0 = pl.multiple_of(w0, PROJ_COLS)
    store(pl.ds(c0, PROJ_COLS), _dot(h_ref[...], w_ref[:, pl.ds(w0, PROJ_COLS)]))


def _conv_rows(ext_sc, wdw_ref, bdw_ref, uc_ref, r0, n_rows):
    lead = HALO_ROWS - (CONV_WIDTH - 1)
    win_rows = n_rows + HALO_ROWS
    r0 = pl.multiple_of(r0, n_rows)
    for c0 in range(0, ext_sc.shape[1], CONV_LANES):
        cols = slice(c0, c0 + CONV_LANES)
        acc = jnp.broadcast_to(bdw_ref[:, cols], (n_rows, CONV_LANES))
        win = ext_sc[pl.ds(r0, win_rows), cols]
        for b in range(SUBLANES):
            shifted = win if b == 0 else pltpu.roll(win, win_rows - b, axis=0)
            for a in range(HALO_ROWS // SUBLANES + 1):
                j = SUBLANES * a + b - lead
                if 0 <= j < CONV_WIDTH:
                    acc = acc + (shifted[SUBLANES * a:SUBLANES * a + n_rows, :]
                                 * wdw_ref[j:j + 1, cols])
        uc_ref[pl.ds(r0, n_rows), cols] = acc


def _inproj_prompt_kernel(x_ref, g_ref, w_ref, wdw_ref, bdw_ref, u_ref, uc_ref, q_ref, k_ref,
                          v_ref, kb_ref, vb_ref, gate_ref, ext_sc, h_sc, *, tm, tiles_per_seq,
                          d_conv, attn_w):
    @pl.when(pl.program_id(0) % tiles_per_seq == 0)
    def _():
        ext_sc[0:HALO_ROWS, :] = jnp.zeros((HALO_ROWS, d_conv), F32)

    h_sc[...] = _rms(x_ref[...], g_ref[...]).astype(BF16)
    u = _glu(h_sc[...], w_ref, d_conv)
    u_ref[...] = u
    ext_sc[HALO_ROWS:, :] = u

    def conv_trip(i, carry):
        _conv_rows(ext_sc, wdw_ref, bdw_ref, uc_ref, i * CONV_ROWS, CONV_ROWS)
        return carry
    lax.fori_loop(0, tm // CONV_ROWS, conv_trip, 0)
    ext_sc[0:HALO_ROWS, :] = ext_sc[tm:tm + HALO_ROWS, :]

    for off, n, store in _qkvg_plan(w_ref, q_ref, k_ref, v_ref, kb_ref, vb_ref, gate_ref,
                                    d_conv=d_conv, attn_w=attn_w):
        for i in range(n):
            _project_piece(h_sc, w_ref, off, i, store)


def _inproj_sample_kernel(x_ref, g_ref, w_ref, u_ref, q_ref, k_ref, v_ref, gate_ref, h_sc,
                          *, d_conv, attn_w):
    h_sc[...] = _rms(x_ref[...], g_ref[...]).astype(BF16)
    u_ref[...] = _glu(h_sc[...], w_ref, d_conv)
    for off, n, store in _qkvg_plan(w_ref, q_ref, k_ref, v_ref, None, None, gate_ref,
                                    d_conv=d_conv, attn_w=attn_w):
        for i in range(n):
            _project_piece(h_sc, w_ref, off, i, store)


def _inproj_prompt(x, g, w, wdw, bdw, *, tm, seq, d_conv, attn_w):
    t, d_model = x.shape
    row = lambda n: pl.BlockSpec((tm, n), lambda i: (i, 0))
    sds = jax.ShapeDtypeStruct
    kern = functools.partial(_inproj_prompt_kernel, tm=tm, tiles_per_seq=seq // tm,
                             d_conv=d_conv, attn_w=attn_w)
    return pl.pallas_call(
        kern,
        grid=(t // tm,),
        in_specs=[row(d_model), _const_spec(g.shape), _const_spec(w.shape),
                  _const_spec(wdw.shape), _const_spec(bdw.shape)],
        out_specs=[row(d_conv), row(d_conv), row(attn_w), row(attn_w), row(attn_w),
                   row(attn_w), row(attn_w), row(2 * d_model)],
        out_shape=[
            sds((t, d_conv), F32),
            sds((t, d_conv), F32),
            sds((t, attn_w), BF16),
            sds((t, attn_w), F32),
            sds((t, attn_w), F32),
            sds((t, attn_w), BF16),
            sds((t, attn_w), BF16),
            sds((t, 2 * d_model), BF16),
        ],
        scratch_shapes=[pltpu.VMEM((tm + HALO_ROWS, d_conv), F32),
                        pltpu.VMEM((tm, d_model), BF16)],
        compiler_params=_params(1),
    )(x, g, w, wdw, bdw)


def _inproj_sample(x, g, w, *, d_conv, attn_w):
    t, d_model = x.shape
    full = lambda n: pl.BlockSpec((t, n), lambda i: (0, 0))
    sds = jax.ShapeDtypeStruct
    kern = functools.partial(_inproj_sample_kernel, d_conv=d_conv, attn_w=attn_w)
    return pl.pallas_call(
        kern,
        grid=(1,),
        in_specs=[full(d_model), _const_spec(g.shape), _const_spec(w.shape)],
        out_specs=[full(d_conv), full(attn_w), full(attn_w), full(attn_w), full(2 * d_model)],
        out_shape=[sds((t, d_conv), F32), sds((t, attn_w), BF16), sds((t, attn_w), F32),
                   sds((t, attn_w), F32), sds((t, 2 * d_model), BF16)],
        scratch_shapes=[pltpu.VMEM((t, d_model), BF16)],
        compiler_params=_params(1),
    )(x, g, w)


def _attn_prompt_kernel(lq_ref, sg_ref, q_ref, k_ref, v_ref, o_ref, vext_sc, qs_sc, s0_sc, s1_sc,
                        m_sc, acc_sc, *, tq, lam_init):
    qi = pl.program_id(2)

    @pl.when(qi == 0)
    def _():
        vext_sc[:, 0:HEAD_WIDTH] = v_ref[0]
        vext_sc[:, HEAD_WIDTH:] = jnp.ones((vext_sc.shape[0], HEAD_WIDTH), BF16)

    q = q_ref[0]
    lane = lax.broadcasted_iota(jnp.int32, q.shape, 1)
    zero = jnp.zeros_like(q)
    qs_sc[0:tq, :] = jnp.where(lane < HEAD_DIM, q, zero)
    qs_sc[tq:, :] = jnp.where(lane >= HEAD_DIM, q, zero)
    m_sc[...] = jnp.full(m_sc.shape, NEG_BIG, F32)
    acc_sc[...] = jnp.zeros(acc_sc.shape, F32)

    def scores(j, s_sc):
        start = pl.multiple_of(j * tq, tq)
        s_sc[...] = _dot_nt(qs_sc[...], k_ref[0, pl.ds(start, tq), :])

    def update(j, s_sc, masked):
        s = s_sc[...]
        if masked:
            r = lax.broadcasted_iota(jnp.int32, s.shape, 0)
            c = lax.broadcasted_iota(jnp.int32, s.shape, 1)
            s = jnp.where(c <= jnp.where(r >= tq, r - tq, r), s, NEG_BIG)
        m = m_sc[...]
        m_new = jnp.maximum(m, jnp.max(s, axis=-1, keepdims=True))
        m_sc[...] = m_new
        alpha = jnp.exp2(m - m_new)
        reps = lambda a, n: jnp.concatenate([a] * (n // HEAD_WIDTH), axis=1)
        p = jnp.exp2((s - reps(m_new, tq)).astype(BF16))
        start = pl.multiple_of(j * tq, tq)
        acc_sc[...] = (reps(alpha, 2 * HEAD_WIDTH) * acc_sc[...]
                       + _dot(p, vext_sc[pl.ds(start, tq), :]))

    scores(0, s0_sc)

    def pair(jj, carry):
        j = 2 * jj
        scores(j + 1, s1_sc)
        update(j, s0_sc, False)
        scores(j + 2, s0_sc)
        update(j + 1, s1_sc, False)
        return carry

    lax.fori_loop(0, qi // 2, pair, 0)

    @pl.when(qi % 2 == 0)
    def _():
        update(qi, s0_sc, True)

    @pl.when(qi % 2 == 1)
    def _():
        scores(qi, s1_sc)
        update(qi - 1, s0_sc, False)
        update(qi, s1_sc, True)

    lam = _lam(lq_ref[...], lam_init)
    acc = acc_sc[...]
    a = acc[:, :HEAD_WIDTH] / acc[:, HEAD_WIDTH:]
    o = a[:tq] - lam * a[tq:]
    o = _rms(o, sg_ref[...]) * (1.0 - lam_init)
    o_ref[0] = o.astype(o_ref.dtype)


def _attn_prompt(lq, sg, q, kb, vb, *, n_heads, tq, lam_init):
    b, s, _ = q.shape
    kern = functools.partial(_attn_prompt_kernel, tq=tq, lam_init=lam_init)
    return pl.pallas_call(
        kern,
        grid=(b, n_heads, s // tq),
        in_specs=[
            pl.BlockSpec(lq.shape, lambda bi, h, i: (0, 0)),
            pl.BlockSpec(sg.shape, lambda bi, h, i: (0, 0)),
            pl.BlockSpec((1, tq, HEAD_WIDTH), lambda bi, h, i: (bi, i, h)),
            pl.BlockSpec((1, s, HEAD_WIDTH), lambda bi, h, i: (bi, 0, h)),
            pl.BlockSpec((1, s, HEAD_WIDTH), lambda bi, h, i: (bi, 0, h)),
        ],
        out_specs=pl.BlockSpec((1, tq, HEAD_WIDTH), lambda bi, h, i: (bi, i, h)),
        out_shape=jax.ShapeDtypeStruct(q.shape, BF16),
        scratch_shapes=[pltpu.VMEM((s, 2 * HEAD_WIDTH), BF16),
                        pltpu.VMEM((2 * tq, HEAD_WIDTH), BF16),
                        pltpu.VMEM((2 * tq, tq), F32),
                        pltpu.VMEM((2 * tq, tq), F32),
                        pltpu.VMEM((2 * tq, HEAD_WIDTH), F32),
                        pltpu.VMEM((2 * tq, 2 * HEAD_WIDTH), F32)],
        compiler_params=_params(3),
    )(lq, sg, q, kb, vb)


def _attn_decode_kernel(pt_ref, lq_ref, sg_ref, q_ref, kn_ref, vn_ref, *rest,
                        n_heads, n_pages_step, lam_init):
    del pt_ref
    k_refs = rest[:n_pages_step]
    v_refs = rest[n_pages_step:2 * n_pages_step]
    o_ref, m_sc, l_sc, acc_sc = rest[2 * n_pages_step:]
    j = pl.program_id(1)
    page_rows = k_refs[0].shape[0] * n_heads

    q = q_ref[0].astype(F32)
    lane = lax.broadcasted_iota(jnp.int32, q.shape, 1)
    qrows_f = jnp.concatenate([jnp.where(lane < HEAD_DIM, q, 0.0),
                               jnp.where(lane >= HEAD_DIM, q, 0.0)], axis=0)
    qrows = qrows_f.astype(BF16)

    @pl.when(j == 0)
    def _():
        kn = kn_ref[0]
        vn = vn_ref[0]
        m_sc[...] = jnp.sum(qrows_f * jnp.concatenate([kn, kn], axis=0), axis=-1, keepdims=True)
        l_sc[...] = jnp.ones_like(l_sc)
        acc_sc[...] = jnp.concatenate([vn, vn], axis=0)

    s = jnp.concatenate(
        [_dot_nt(qrows, kr[...].reshape(page_rows, HEAD_WIDTH).astype(BF16)) for kr in k_refs],
        axis=1)
    r = lax.broadcasted_iota(jnp.int32, s.shape, 0)
    c = lax.broadcasted_iota(jnp.int32, s.shape, 1)
    s = jnp.where((c % n_heads) == (r % n_heads), s, NEG_BIG)
    m = m_sc[...]
    m_new = jnp.maximum(m, jnp.max(s, axis=-1, keepdims=True))
    alpha = jnp.exp2(m - m_new)
    p = jnp.exp2(s - m_new)
    l_sc[...] = alpha * l_sc[...] + jnp.sum(p, axis=-1, keepdims=True)
    m_sc[...] = m_new
    pb = p.astype(BF16)
    pv = None
    for i, vr in enumerate(v_refs):
        part = _dot(pb[:, i * page_rows:(i + 1) * page_rows],
                    vr[...].reshape(page_rows, HEAD_WIDTH).astype(BF16))
        pv = part if pv is None else pv + part
    acc_sc[...] = alpha * acc_sc[...] + pv

    @pl.when(j == pl.num_programs(1) - 1)
    def _():
        lam = _lam(lq_ref[...], lam_init)
        a = acc_sc[...] / l_sc[...]
        o = a[:n_heads] - lam * a[n_heads:]
        o = _rms(o, sg_ref[...]) * (1.0 - lam_init)
        o_ref[0] = o.astype(o_ref.dtype)


def _attn_decode(page_table, lq, sg, q, kn, vn, cache_k, cache_v, *, layer, pages_step, lam_init):
    db = q.shape[0]
    _, _, page_size, n_heads, head_w = cache_k.shape
    n_pages = page_table.shape[1]
    rows = 2 * n_heads
    tok = pl.BlockSpec((1, n_heads, head_w), lambda b, j, pt: (b, 0, 0))

    def page_spec(i):
        return pl.BlockSpec((None, None, page_size, n_heads, head_w),
                            lambda b, j, pt: (layer, pt[b, j * pages_step + i], 0, 0, 0))

    kern = functools.partial(_attn_decode_kernel, n_heads=n_heads,
                             n_pages_step=pages_step, lam_init=lam_init)
    return pl.pallas_call(
        kern,
        grid_spec=pltpu.PrefetchScalarGridSpec(
            num_scalar_prefetch=1,
            grid=(db, n_pages // pages_step),
            in_specs=[pl.BlockSpec(lq.shape, lambda b, j, pt: (0, 0)),
                      pl.BlockSpec(sg.shape, lambda b, j, pt: (0, 0)),
                      tok, tok, tok]
                     + [page_spec(i) for i in range(pages_step)]
                     + [page_spec(i) for i in range(pages_step)],
            out_specs=tok,
            scratch_shapes=[pltpu.VMEM((rows, 1), F32), pltpu.VMEM((rows, 1), F32),
                            pltpu.VMEM((rows, head_w), F32)],
        ),
        out_shape=jax.ShapeDtypeStruct((db, n_heads, head_w), BF16),
        compiler_params=_params(2),
    )(page_table, lq, sg, q, kn, vn, *([cache_k] * pages_step), *([cache_v] * pages_step))


def _mix_and_mlp(x, uc, o, gate, lng, lnb, wco, wao, wo, wup, wdn, n_post, n_pre, n_mpost,
                 *, d_model, ff_chunk):
    mu = jnp.mean(uc, axis=-1, keepdims=True)
    dc = uc - mu
    var = jnp.mean(jnp.square(dc), axis=-1, keepdims=True)
    ln = dc * lax.rsqrt(var + LN_EPS) * lng[...] + lnb[...]
    y_conv = _dot((ln * jax.nn.sigmoid(ln)).astype(BF16), wco[...])
    y_attn = _dot(o, wao[...])
    mix = gate[:, :d_model].astype(F32) * y_conv + gate[:, d_model:].astype(F32) * y_attn
    x1 = x + _rms(_dot(mix.astype(BF16), wo[...]), n_post[...])
    h2 = _rms(x1, n_pre[...]).astype(BF16)
    d_ff = wup.shape[1]
    f = None
    for c0 in range(0, d_ff, ff_chunk):
        a = jnp.square(jnp.maximum(_dot(h2, wup[:, c0:c0 + ff_chunk]), 0.0)).astype(BF16)
        part = _dot(a, wdn[c0:c0 + ff_chunk, :])
        f = part if f is None else f + part
    return x1 + _rms(f, n_mpost[...])


def _post_prompt_kernel(x_ref, uc_ref, o_ref, gate_ref, lng, lnb, wco, wao, wo, wup, wdn,
                        n_post, n_pre, n_mpost, out_ref, *, d_model, ff_chunk):
    out_ref[...] = _mix_and_mlp(x_ref[...], uc_ref[...], o_ref[...], gate_ref[...], lng, lnb,
                                wco, wao, wo, wup, wdn, n_post, n_pre, n_mpost,
                                d_model=d_model, ff_chunk=ff_chunk)


def _post_sample_kernel(x_ref, u_ref, st_ref, o_ref, gate_ref, wdw_ref, bdw_ref, lng, lnb,
                        wco, wao, wo, wup, wdn, n_post, n_pre, n_mpost, out_ref,
                        *, d_model, ff_chunk):
    uc = bdw_ref[...] + u_ref[...] * wdw_ref[CONV_WIDTH - 1:CONV_WIDTH, :]
    for j in range(CONV_WIDTH - 1):
        uc = uc + st_ref[j] * wdw_ref[j:j + 1, :]
    out_ref[...] = _mix_and_mlp(x_ref[...], uc, o_ref[...], gate_ref[...], lng, lnb,
                                wco, wao, wo, wup, wdn, n_post, n_pre, n_mpost,
                                d_model=d_model, ff_chunk=ff_chunk)


def _weight_specs(ws):
    return [_const_spec(w.shape) for w in ws]


def _post_prompt(x, uc, o, gate, weights, *, tm, ff_chunk):
    t, d_model = x.shape
    row = lambda n: pl.BlockSpec((tm, n), lambda i: (i, 0))
    kern = functools.partial(_post_prompt_kernel, d_model=d_model, ff_chunk=ff_chunk)
    return pl.pallas_call(
        kern,
        grid=(t // tm,),
        in_specs=[row(d_model), row(uc.shape[1]), row(o.shape[1]), row(2 * d_model)]
                 + _weight_specs(weights),
        out_specs=row(d_model),
        out_shape=jax.ShapeDtypeStruct((t, d_model), F32),
        compiler_params=_params(1),
    )(x, uc, o, gate, *weights)


def _post_sample(x, u, state_t, o, gate, conv_w, weights, *, ff_chunk):
    t, d_model = x.shape
    full = lambda a: pl.BlockSpec(a.shape, lambda i: (0,) * a.ndim)
    kern = functools.partial(_post_sample_kernel, d_model=d_model, ff_chunk=ff_chunk)
    return pl.pallas_call(
        kern,
        grid=(1,),
        in_specs=[full(x), full(u), full(state_t), full(o), full(gate)]
                 + _weight_specs(conv_w + weights),
        out_specs=pl.BlockSpec((t, d_model), lambda i: (0, 0)),
        out_shape=jax.ShapeDtypeStruct((t, d_model), F32),
        compiler_params=_params(1),
    )(x, u, state_t, o, gate, *conv_w, *weights)


def kernel(x_prompt, x_sample, cache_k, cache_v, state_conv, page_table, norm_mix_pre, norm_mix_post, norm_mlp_pre, norm_mlp_post, w_in, w_dw, b_dw, conv_ln_g, conv_ln_b, w_conv_out, lambda_qk, subln_g, w_attn_out, w_out, w_up, w_down):
    batch, seq, d_model = x_prompt.shape
    dec_batch, dec_seq, _ = x_sample.shape
    assert dec_seq == 1
    depth, _, _, n_heads, head_w = cache_k.shape
    assert head_w == HEAD_WIDTH and state_conv.shape[2] == CONV_WIDTH - 1
    d_conv = state_conv.shape[3]
    attn_w = n_heads * HEAD_WIDTH
    ff_chunk = d_model

    xp = x_prompt.reshape(batch * seq, d_model)
    xs = x_sample.reshape(dec_batch, d_model)
    row = lambda a: a.reshape(1, -1)

    kp, vp, cp, ks, vs, cs = [], [], [], [], [], []
    for l in range(depth):
        lam_init = 0.8 - 0.6 * math.exp(-0.3 * l)
        w_in_b = w_in[l].astype(BF16)
        conv_w = (w_dw[l], row(b_dw[l]))
        weights = (row(conv_ln_g[l]), row(conv_ln_b[l]),
                   w_conv_out[l].astype(BF16), w_attn_out[l].astype(BF16),
                   w_out[l].astype(BF16), w_up[l].astype(BF16), w_down[l].astype(BF16),
                   row(norm_mix_post[l]), row(norm_mlp_pre[l]), row(norm_mlp_post[l]))
        g_pre = row(norm_mix_pre[l])
        lq = lambda_qk[l]
        sg = row(subln_g[l])

        u, uc, q, k, v, kb, vb, gate = _inproj_prompt(xp, g_pre, w_in_b, *conv_w, tm=TM_IN,
                                                      seq=seq, d_conv=d_conv, attn_w=attn_w)
        o = _attn_prompt(lq, sg, q.reshape(batch, seq, attn_w), kb.reshape(batch, seq, attn_w),
                         vb.reshape(batch, seq, attn_w), n_heads=n_heads, tq=TQ,
                         lam_init=lam_init)
        xp = _post_prompt(xp, uc, o.reshape(batch * seq, attn_w), gate, weights, tm=TM_POST,
                          ff_chunk=ff_chunk)
        kp.append(k.reshape(batch, seq, n_heads, HEAD_WIDTH))
        vp.append(v.reshape(batch, seq, n_heads, HEAD_WIDTH))
        cp.append(u.reshape(batch, seq, d_conv)[:, seq - (CONV_WIDTH - 1):])

        u, q, k, v, gate = _inproj_sample(xs, g_pre, w_in_b, d_conv=d_conv, attn_w=attn_w)
        heads = lambda a: a.reshape(dec_batch, n_heads, HEAD_WIDTH)
        o = _attn_decode(page_table, lq, sg, heads(q), heads(k), heads(v), cache_k, cache_v,
                         layer=l, pages_step=DEC_PAGES, lam_init=lam_init)
        st = state_conv[l]
        xs = _post_sample(xs, u, st.transpose(1, 0, 2), o.reshape(dec_batch, attn_w), gate,
                          conv_w, weights, ff_chunk=ff_chunk)
        ks.append(k.reshape(dec_batch, 1, n_heads, HEAD_WIDTH))
        vs.append(v.reshape(dec_batch, 1, n_heads, HEAD_WIDTH))
        cs.append(jnp.concatenate([st[:, 1:], u[:, None, :]], axis=1))

    return (xp.reshape(batch, seq, d_model), xs.reshape(dec_batch, 1, d_model),
            jnp.stack(kp), jnp.stack(vp), jnp.stack(cp),
            jnp.stack(ks), jnp.stack(vs), jnp.stack(cs))
```

```python
import functools
import math

import jax
import jax.numpy as jnp
from jax import lax
from jax.experimental import pallas as pl
from jax.experimental.pallas import tpu as pltpu

F32 = jnp.float32
BF16 = jnp.bfloat16

HEAD_DIM = 64
HEAD_WIDTH = 2 * HEAD_DIM
CONV_WIDTH = 31
SUBLANES = 8
HALO_ROWS = 32
NORM_EPS = 1e-6
LN_EPS = 1e-5
NEG_BIG = -1e30
Q_SCALE = math.log2(math.e) / math.sqrt(HEAD_DIM)
VMEM_LIMIT = 60 * 1024 * 1024

TM_IN = 512
TM_POST = 512
TQ = 512
CONV_ROWS = 64
CONV_LANES = 128
PROJ_COLS = 512


def _dot(a, b):
    return jnp.dot(a, b, preferred_element_type=F32)


def _dot_nt(a, b):
    return lax.dot_general(a, b, (((1,), (1,)), ((), ())), preferred_element_type=F32)


def _rms(x, g):
    y = x * lax.rsqrt(jnp.mean(jnp.square(x), axis=-1, keepdims=True) + NORM_EPS)
    return y * g


def _lam(lq, lam_init):
    a = jnp.sum(lq[0:1] * lq[1:2], axis=-1, keepdims=True)
    b = jnp.sum(lq[2:3] * lq[3:4], axis=-1, keepdims=True)
    return jnp.exp(a) - jnp.exp(b) + lam_init


def _const_spec(shape):
    n = len(shape)
    return pl.BlockSpec(shape, lambda *_: (0,) * n, pipeline_mode=pl.Buffered(1))


def _params(n_grid):
    return pltpu.CompilerParams(dimension_semantics=("arbitrary",) * n_grid,
                                vmem_limit_bytes=VMEM_LIMIT)


def _glu(h, w_ref, d_conv):
    zab = _dot(h, w_ref[:, 0:2 * d_conv])
    return zab[:, :d_conv] * jax.nn.sigmoid(zab[:, d_conv:])


def _store_q(q_ref, cs, z):
    q_ref[:, cs] = (z * Q_SCALE).astype(BF16)


def _store_kv(f_ref, b_ref, cs, z):
    f_ref[:, cs] = z
    if b_ref is not None:
        b_ref[:, cs] = z.astype(BF16)


def _store_gate(gate_ref, cs, z):
    gate_ref[:, cs] = jax.nn.sigmoid(z).astype(BF16)


def _qkvg_plan(w_ref, q_ref, k_ref, v_ref, kb_ref, vb_ref, gate_ref, *, d_conv, attn_w):
    off_q = 2 * d_conv
    off_k = off_q + attn_w
    off_v = off_k + attn_w
    off_g = off_v + attn_w
    return [(off_q, attn_w // PROJ_COLS, functools.partial(_store_q, q_ref)),
            (off_k, attn_w // PROJ_COLS, functools.partial(_store_kv, k_ref, kb_ref)),
            (off_v, attn_w // PROJ_COLS, functools.partial(_store_kv, v_ref, vb_ref)),
            (off_g, (w_ref.shape[1] - off_g) // PROJ_COLS,
             functools.partial(_store_gate, gate_ref))]


def _project_piece(h_ref, w_ref, off, i, store):
    c0 = i * PROJ_COLS
    w0 = off + c0
    if not isinstance(i, int):
        c0 = pl.multiple_of(c0, PROJ_COLS)
        w0 = pl.multiple_of(w0, PROJ_COLS)
    store(pl.ds(c0, PROJ_COLS), _dot(h_ref[...], w_ref[:, pl.ds(w0, PROJ_COLS)]))


def _conv_rows(ext_sc, wdw_ref, bdw_ref, uc_ref, r0, n_rows):
    lead = HALO_ROWS - (CONV_WIDTH - 1)
    win_rows = n_rows + HALO_ROWS
    r0 = pl.multiple_of(r0, n_rows)
    for c0 in range(0, ext_sc.shape[1], CONV_LANES):
        cols = slice(c0, c0 + CONV_LANES)
        acc = jnp.broadcast_to(bdw_ref[:, cols], (n_rows, CONV_LANES))
        win = ext_sc[pl.ds(r0, win_rows), cols]
        for b in range(SUBLANES):
            shifted = win if b == 0 else pltpu.roll(win, win_rows - b, axis=0)
            for a in range(HALO_ROWS // SUBLANES + 1):
                j = SUBLANES * a + b - lead
                if 0 <= j < CONV_WIDTH:
                    acc = acc + (shifted[SUBLANES * a:SUBLANES * a + n_rows, :]
                                 * wdw_ref[j:j + 1, cols])
        uc_ref[pl.ds(r0, n_rows), cols] = acc


def _inproj_prompt_kernel(x_ref, g_ref, w_ref, wdw_ref, bdw_ref, u_ref, uc_ref, q_ref, k_ref,
                          v_ref, kb_ref, vb_ref, gate_ref, ext_sc, h_sc, *, tm, tiles_per_seq,
                          d_conv, attn_w):
    @pl.when(pl.program_id(0) % tiles_per_seq == 0)
    def _():
        ext_sc[0:HALO_ROWS, :] = jnp.zeros((HALO_ROWS, d_conv), F32)

    h_sc[...] = _rms(x_ref[...], g_ref[...]).astype(BF16)
    u = _glu(h_sc[...], w_ref, d_conv)
    u_ref[...] = u
    ext_sc[HALO_ROWS:, :] = u

    def conv_trip(i, carry):
        _conv_rows(ext_sc, wdw_ref, bdw_ref, uc_ref, i * CONV_ROWS, CONV_ROWS)
        return carry
    lax.fori_loop(0, tm // CONV_ROWS, conv_trip, 0)
    ext_sc[0:HALO_ROWS, :] = ext_sc[tm:tm + HALO_ROWS, :]

    for off, n, store in _qkvg_plan(w_ref, q_ref, k_ref, v_ref, kb_ref, vb_ref, gate_ref,
                                    d_conv=d_conv, attn_w=attn_w):
        for i in range(n):
            _project_piece(h_sc, w_ref, off, i, store)


def _inproj_sample_kernel(x_ref, g_ref, w_ref, u_ref, q_ref, k_ref, v_ref, gate_ref, h_sc,
                          *, d_conv, attn_w):
    h_sc[...] = _rms(x_ref[...], g_ref[...]).astype(BF16)
    u_ref[...] = _glu(h_sc[...], w_ref, d_conv)
    for off, n, store in _qkvg_plan(w_ref, q_ref, k_ref, v_ref, None, None, gate_ref,
                                    d_conv=d_conv, attn_w=attn_w):
        for i in range(n):
            _project_piece(h_sc, w_ref, off, i, store)


def _inproj_prompt(x, g, w, wdw, bdw, *, tm, seq, d_conv, attn_w):
    t, d_model = x.shape
    row = lambda n: pl.BlockSpec((tm, n), lambda i: (i, 0))
    sds = jax.ShapeDtypeStruct
    kern = functools.partial(_inproj_prompt_kernel, tm=tm, tiles_per_seq=seq // tm,
                             d_conv=d_conv, attn_w=attn_w)
    return pl.pallas_call(
        kern,
        grid=(t // tm,),
        in_specs=[row(d_model), _const_spec(g.shape), _const_spec(w.shape),
                  _const_spec(wdw.shape), _const_spec(bdw.shape)],
        out_specs=[row(d_conv), row(d_conv), row(attn_w), row(attn_w), row(attn_w),
                   row(attn_w), row(attn_w), row(2 * d_model)],
        out_shape=[
            sds((t, d_conv), F32),
            sds((t, d_conv), F32),
            sds((t, attn_w), BF16),
            sds((t, attn_w), F32),
            sds((t, attn_w), F32),
            sds((t, attn_w), BF16),
            sds((t, attn_w), BF16),
            sds((t, 2 * d_model), BF16),
        ],
        scratch_shapes=[pltpu.VMEM((tm + HALO_ROWS, d_conv), F32),
                        pltpu.VMEM((tm, d_model), BF16)],
        compiler_params=_params(1),
    )(x, g, w, wdw, bdw)


def _inproj_sample(x, g, w, *, d_conv, attn_w):
    t, d_model = x.shape
    full = lambda n: pl.BlockSpec((t, n), lambda i: (0, 0))
    sds = jax.ShapeDtypeStruct
    kern = functools.partial(_inproj_sample_kernel, d_conv=d_conv, attn_w=attn_w)
    return pl.pallas_call(
        kern,
        grid=(1,),
        in_specs=[full(d_model), _const_spec(g.shape), _const_spec(w.shape)],
        out_specs=[full(d_conv), full(attn_w), full(attn_w), full(attn_w), full(2 * d_model)],
        out_shape=[sds((t, d_conv), F32), sds((t, attn_w), BF16), sds((t, attn_w), F32),
                   sds((t, attn_w), F32), sds((t, 2 * d_model), BF16)],
        scratch_shapes=[pltpu.VMEM((t, d_model), BF16)],
        compiler_params=_params(1),
    )(x, g, w)


def _prompt_attention_step(qi, lq_ref, sg_ref, q_ref, k_ref, o_ref, vext_sc, qs_sc, s0_sc, s1_sc,
                           m_sc, acc_sc, *, tq, lam_init):
    q = q_ref[0]
    lane = lax.broadcasted_iota(jnp.int32, q.shape, 1)
    zero = jnp.zeros_like(q)
    qs_sc[0:tq, :] = jnp.where(lane < HEAD_DIM, q, zero)
    qs_sc[tq:, :] = jnp.where(lane >= HEAD_DIM, q, zero)
    m_sc[...] = jnp.full(m_sc.shape, NEG_BIG, F32)
    acc_sc[...] = jnp.zeros(acc_sc.shape, F32)

    def scores(j, s_sc):
        start = pl.multiple_of(j * tq, tq)
        s_sc[...] = _dot_nt(qs_sc[...], k_ref[0, pl.ds(start, tq), :])

    def update(j, s_sc, masked):
        s = s_sc[...]
        if masked:
            r = lax.broadcasted_iota(jnp.int32, s.shape, 0)
            c = lax.broadcasted_iota(jnp.int32, s.shape, 1)
            s = jnp.where(c <= jnp.where(r >= tq, r - tq, r), s, NEG_BIG)
        m = m_sc[...]
        m_new = jnp.maximum(m, jnp.max(s, axis=-1, keepdims=True))
        m_sc[...] = m_new
        alpha = jnp.exp2(m - m_new)
        reps = lambda a, n: jnp.concatenate([a] * (n // HEAD_WIDTH), axis=1)
        p = jnp.exp2((s - reps(m_new, tq)).astype(BF16))
        start = pl.multiple_of(j * tq, tq)
        acc_sc[...] = (reps(alpha, 2 * HEAD_WIDTH) * acc_sc[...]
                       + _dot(p, vext_sc[pl.ds(start, tq), :]))

    scores(0, s0_sc)

    def pair(jj, carry):
        j = 2 * jj
        scores(j + 1, s1_sc)
        update(j, s0_sc, False)
        scores(j + 2, s0_sc)
        update(j + 1, s1_sc, False)
        return carry

    lax.fori_loop(0, qi // 2, pair, 0)

    @pl.when(qi % 2 == 0)
    def _():
        update(qi, s0_sc, True)

    @pl.when(qi % 2 == 1)
    def _():
        scores(qi, s1_sc)
        update(qi - 1, s0_sc, False)
        update(qi, s1_sc, True)

    lam = _lam(lq_ref[...], lam_init)
    acc = acc_sc[...]
    a = acc[:, :HEAD_WIDTH] / acc[:, HEAD_WIDTH:]
    o = a[:tq] - lam * a[tq:]
    o = _rms(o, sg_ref[...]) * (1.0 - lam_init)
    o_ref[0] = o.astype(o_ref.dtype)


def _stacked_q_rows(q_ref):
    q = q_ref[0].astype(F32)
    lane = lax.broadcasted_iota(jnp.int32, q.shape, 1)
    return jnp.concatenate([jnp.where(lane < HEAD_DIM, q, 0.0),
                            jnp.where(lane >= HEAD_DIM, q, 0.0)], axis=0)


def _decode_attention_init(q_ref, kn_ref, vn_ref, m_sc, l_sc, acc_sc):
    kn = kn_ref[0]
    vn = vn_ref[0]
    m_sc[...] = jnp.sum(_stacked_q_rows(q_ref) * jnp.concatenate([kn, kn], axis=0),
                        axis=-1, keepdims=True)
    l_sc[...] = jnp.ones_like(l_sc)
    acc_sc[...] = jnp.concatenate([vn, vn], axis=0)


def _decode_attention_step(lq_ref, sg_ref, q_ref, k_refs, v_refs, o_ref, m_sc, l_sc, acc_sc,
                           *, n_heads, lam_init):
    page_rows = k_refs[0].shape[0] * n_heads
    qrows = _stacked_q_rows(q_ref).astype(BF16)

    s = jnp.concatenate(
        [_dot_nt(qrows, kr[...].reshape(page_rows, HEAD_WIDTH).astype(BF16)) for kr in k_refs],
        axis=1)
    r = lax.broadcasted_iota(jnp.int32, s.shape, 0)
    c = lax.broadcasted_iota(jnp.int32, s.shape, 1)
    s = jnp.where((c % n_heads) == (r % n_heads), s, NEG_BIG)
    m = m_sc[...]
    m_new = jnp.maximum(m, jnp.max(s, axis=-1, keepdims=True))
    alpha = jnp.exp2(m - m_new)
    p = jnp.exp2(s - m_new)
    l = alpha * l_sc[...] + jnp.sum(p, axis=-1, keepdims=True)
    l_sc[...] = l
    m_sc[...] = m_new
    pb = p.astype(BF16)
    pv = None
    for i, vr in enumerate(v_refs):
        part = _dot(pb[:, i * page_rows:(i + 1) * page_rows],
                    vr[...].reshape(page_rows, HEAD_WIDTH).astype(BF16))
        pv = part if pv is None else pv + part
    acc = alpha * acc_sc[...] + pv
    acc_sc[...] = acc

    a = acc / l
    o = a[:n_heads] - _lam(lq_ref[...], lam_init) * a[n_heads:]
    o = _rms(o, sg_ref[...]) * (1.0 - lam_init)
    o_ref[0] = o.astype(o_ref.dtype)


def _attention_kernel(pt_ref, lq_ref, sg_ref, q_ref, k_ref, v_ref, qd_ref, kn_ref, vn_ref, *rest,
                      n_heads, n_pages_step, steps_per_seq, tq, lam_init):
    del pt_ref
    k_pages = rest[:n_pages_step]
    v_pages = rest[n_pages_step:2 * n_pages_step]
    (o_ref, od_ref, vext_sc, qs_sc, s0_sc, s1_sc, m_sc, acc_sc,
     md_sc, ld_sc, accd_sc) = rest[2 * n_pages_step:]
    qi = pl.program_id(2)
    step = (pl.program_id(0) * pl.num_programs(1) + pl.program_id(1)) * pl.num_programs(2) + qi

    @pl.when(step % steps_per_seq == 0)
    def _():
        _decode_attention_init(qd_ref, kn_ref, vn_ref, md_sc, ld_sc, accd_sc)

    @pl.when(qi == 0)
    def _():
        vext_sc[:, 0:HEAD_WIDTH] = v_ref[0]
        vext_sc[:, HEAD_WIDTH:] = jnp.ones((vext_sc.shape[0], HEAD_WIDTH), BF16)

    _decode_attention_step(lq_ref, sg_ref, qd_ref, k_pages, v_pages, od_ref, md_sc, ld_sc, accd_sc,
                           n_heads=n_heads, lam_init=lam_init)
    _prompt_attention_step(qi, lq_ref, sg_ref, q_ref, k_ref, o_ref, vext_sc, qs_sc, s0_sc, s1_sc,
                           m_sc, acc_sc, tq=tq, lam_init=lam_init)


def _attention(page_table, lq, sg, q, kb, vb, qd, kn, vn, cache_k, cache_v, *, layer, tq,
               lam_init):
    b, s, _ = q.shape
    db = qd.shape[0]
    _, _, page_size, n_heads, head_w = cache_k.shape
    n_pages = page_table.shape[1]
    n_steps = b * n_heads * (s // tq)
    pages_step = db * n_pages // n_steps
    assert pages_step * n_steps == db * n_pages and n_pages % pages_step == 0
    steps_per_seq = n_pages // pages_step

    def step_of(bi, h, i):
        return (bi * n_heads + h) * (s // tq) + i

    const = lambda a: pl.BlockSpec(a.shape, lambda bi, h, i, pt: (0,) * a.ndim)
    tok = pl.BlockSpec((1, n_heads, head_w),
                       lambda bi, h, i, pt: (step_of(bi, h, i) // steps_per_seq, 0, 0))
    head_rows = lambda rows, blk: pl.BlockSpec((1, rows, HEAD_WIDTH),
                                               lambda bi, h, i, pt: (bi, blk(i), h))

    def page_spec(n):
        def index(bi, h, i, pt):
            t = step_of(bi, h, i)
            return (layer, pt[t // steps_per_seq, (t % steps_per_seq) * pages_step + n], 0, 0, 0)
        return pl.BlockSpec((None, None, page_size, n_heads, head_w), index)

    kern = functools.partial(_attention_kernel, n_heads=n_heads, n_pages_step=pages_step,
                             steps_per_seq=steps_per_seq, tq=tq, lam_init=lam_init)
    rows = 2 * n_heads
    return pl.pallas_call(
        kern,
        grid_spec=pltpu.PrefetchScalarGridSpec(
            num_scalar_prefetch=1,
            grid=(b, n_heads, s // tq),
            in_specs=[const(lq), const(sg),
                      head_rows(tq, lambda i: i), head_rows(s, lambda i: 0),
                      head_rows(s, lambda i: 0), tok, tok, tok]
                     + [page_spec(n) for n in range(pages_step)]
                     + [page_spec(n) for n in range(pages_step)],
            out_specs=[head_rows(tq, lambda i: i), tok],
            scratch_shapes=[pltpu.VMEM((s, 2 * HEAD_WIDTH), BF16),
                            pltpu.VMEM((2 * tq, HEAD_WIDTH), BF16),
                            pltpu.VMEM((2 * tq, tq), F32),
                            pltpu.VMEM((2 * tq, tq), F32),
                            pltpu.VMEM((2 * tq, HEAD_WIDTH), F32),
                            pltpu.VMEM((2 * tq, 2 * HEAD_WIDTH), F32),
                            pltpu.VMEM((rows, 1), F32),
                            pltpu.VMEM((rows, 1), F32),
                            pltpu.VMEM((rows, head_w), F32)],
        ),
        out_shape=[jax.ShapeDtypeStruct(q.shape, BF16),
                   jax.ShapeDtypeStruct((db, n_heads, head_w), BF16)],
        compiler_params=_params(3),
    )(page_table, lq, sg, q, kb, vb, qd, kn, vn,
      *([cache_k] * pages_step), *([cache_v] * pages_step))


def _mix_and_mlp(x, uc, o, gate, lng, lnb, wco, wao, wo, wup, wdn, n_post, n_pre, n_mpost,
                 *, d_model, ff_chunk):
    mu = jnp.mean(uc, axis=-1, keepdims=True)
    dc = uc - mu
    var = jnp.mean(jnp.square(dc), axis=-1, keepdims=True)
    ln = dc * lax.rsqrt(var + LN_EPS) * lng[...] + lnb[...]
    y_conv = _dot((ln * jax.nn.sigmoid(ln)).astype(BF16), wco[...])
    y_attn = _dot(o, wao[...])
    mix = gate[:, :d_model].astype(F32) * y_conv + gate[:, d_model:].astype(F32) * y_attn
    x1 = x + _rms(_dot(mix.astype(BF16), wo[...]), n_post[...])
    h2 = _rms(x1, n_pre[...]).astype(BF16)
    d_ff = wup.shape[1]
    f = None
    for c0 in range(0, d_ff, ff_chunk):
        a = jnp.square(jnp.maximum(_dot(h2, wup[:, c0:c0 + ff_chunk]), 0.0)).astype(BF16)
        part = _dot(a, wdn[c0:c0 + ff_chunk, :])
        f = part if f is None else f + part
    return x1 + _rms(f, n_mpost[...])


def _post_prompt_kernel(x_ref, uc_ref, o_ref, gate_ref, lng, lnb, wco, wao, wo, wup, wdn,
                        n_post, n_pre, n_mpost, out_ref, *, d_model, ff_chunk):
    out_ref[...] = _mix_and_mlp(x_ref[...], uc_ref[...], o_ref[...], gate_ref[...], lng, lnb,
                                wco, wao, wo, wup, wdn, n_post, n_pre, n_mpost,
                                d_model=d_model, ff_chunk=ff_chunk)


def _post_sample_kernel(x_ref, u_ref, st_ref, o_ref, gate_ref, wdw_ref, bdw_ref, lng, lnb,
                        wco, wao, wo, wup, wdn, n_post, n_pre, n_mpost, out_ref,
                        *, d_model, ff_chunk):
    uc = bdw_ref[...] + u_ref[...] * wdw_ref[CONV_WIDTH - 1:CONV_WIDTH, :]
    for j in range(CONV_WIDTH - 1):
        uc = uc + st_ref[j] * wdw_ref[j:j + 1, :]
    out_ref[...] = _mix_and_mlp(x_ref[...], uc, o_ref[...], gate_ref[...], lng, lnb,
                                wco, wao, wo, wup, wdn, n_post, n_pre, n_mpost,
                                d_model=d_model, ff_chunk=ff_chunk)


def _weight_specs(ws):
    return [_const_spec(w.shape) for w in ws]


def _post_prompt(x, uc, o, gate, weights, *, tm, ff_chunk):
    t, d_model = x.shape
    row = lambda n: pl.BlockSpec((tm, n), lambda i: (i, 0))
    kern = functools.partial(_post_prompt_kernel, d_model=d_model, ff_chunk=ff_chunk)
    return pl.pallas_call(
        kern,
        grid=(t // tm,),
        in_specs=[row(d_model), row(uc.shape[1]), row(o.shape[1]), row(2 * d_model)]
                 + _weight_specs(weights),
        out_specs=row(d_model),
        out_shape=jax.ShapeDtypeStruct((t, d_model), F32),
        compiler_params=_params(1),
    )(x, uc, o, gate, *weights)


def _post_sample(x, u, state_t, o, gate, conv_w, weights, *, ff_chunk):
    t, d_model = x.shape
    full = lambda a: pl.BlockSpec(a.shape, lambda i: (0,) * a.ndim)
    kern = functools.partial(_post_sample_kernel, d_model=d_model, ff_chunk=ff_chunk)
    return pl.pallas_call(
        kern,
        grid=(1,),
        in_specs=[full(x), full(u), full(state_t), full(o), full(gate)]
                 + _weight_specs(conv_w + weights),
        out_specs=pl.BlockSpec((t, d_model), lambda i: (0, 0)),
        out_shape=jax.ShapeDtypeStruct((t, d_model), F32),
        compiler_params=_params(1),
    )(x, u, state_t, o, gate, *conv_w, *weights)


def kernel(x_prompt, x_sample, cache_k, cache_v, state_conv, page_table, norm_mix_pre, norm_mix_post, norm_mlp_pre, norm_mlp_post, w_in, w_dw, b_dw, conv_ln_g, conv_ln_b, w_conv_out, lambda_qk, subln_g, w_attn_out, w_out, w_up, w_down):
    batch, seq, d_model = x_prompt.shape
    dec_batch, dec_seq, _ = x_sample.shape
    assert dec_seq == 1
    depth, _, _, n_heads, head_w = cache_k.shape
    assert head_w == HEAD_WIDTH and state_conv.shape[2] == CONV_WIDTH - 1
    d_conv = state_conv.shape[3]
    attn_w = n_heads * HEAD_WIDTH
    ff_chunk = d_model

    xp = x_prompt.reshape(batch * seq, d_model)
    xs = x_sample.reshape(dec_batch, d_model)
    row = lambda a: a.reshape(1, -1)

    kp, vp, cp, ks, vs, cs = [], [], [], [], [], []
    for l in range(depth):
        lam_init = 0.8 - 0.6 * math.exp(-0.3 * l)
        w_in_b = w_in[l].astype(BF16)
        conv_w = (w_dw[l], row(b_dw[l]))
        weights = (row(conv_ln_g[l]), row(conv_ln_b[l]),
                   w_conv_out[l].astype(BF16), w_attn_out[l].astype(BF16),
                   w_out[l].astype(BF16), w_up[l].astype(BF16), w_down[l].astype(BF16),
                   row(norm_mix_post[l]), row(norm_mlp_pre[l]), row(norm_mlp_post[l]))
        g_pre = row(norm_mix_pre[l])
        lq = lambda_qk[l]
        sg = row(subln_g[l])

        u, uc, q, k, v, kb, vb, gate = _inproj_prompt(xp, g_pre, w_in_b, *conv_w, tm=TM_IN,
                                                      seq=seq, d_conv=d_conv, attn_w=attn_w)
        us, qs, k_s, v_s, gate_s = _inproj_sample(xs, g_pre, w_in_b, d_conv=d_conv,
                                                  attn_w=attn_w)
        seqs = lambda a: a.reshape(batch, seq, attn_w)
        heads = lambda a: a.reshape(dec_batch, n_heads, HEAD_WIDTH)
        o, o_s = _attention(page_table, lq, sg, seqs(q), seqs(kb), seqs(vb), heads(qs),
                            heads(k_s), heads(v_s), cache_k, cache_v, layer=l, tq=TQ,
                            lam_init=lam_init)

        xp = _post_prompt(xp, uc, o.reshape(batch * seq, attn_w), gate, weights, tm=TM_POST,
                          ff_chunk=ff_chunk)
        kp.append(k.reshape(batch, seq, n_heads, HEAD_WIDTH))
        vp.append(v.reshape(batch, seq, n_heads, HEAD_WIDTH))
        cp.append(u.reshape(batch, seq, d_conv)[:, seq - (CONV_WIDTH - 1):])

        st = state_conv[l]
        xs = _post_sample(xs, us, st.transpose(1, 0, 2), o_s.reshape(dec_batch, attn_w), gate_s,
                          conv_w, weights, ff_chunk=ff_chunk)
        ks.append(k_s.reshape(dec_batch, 1, n_heads, HEAD_WIDTH))
        vs.append(v_s.reshape(dec_batch, 1, n_heads, HEAD_WIDTH))
        cs.append(jnp.concatenate([st[:, 1:], us[:, None, :]], axis=1))

    return (xp.reshape(batch, seq, d_model), xs.reshape(dec_batch, 1, d_model),
            jnp.stack(kp), jnp.stack(vp), jnp.stack(cp),
            jnp.stack(ks), jnp.stack(vs), jnp.stack(cs))
```

```python
import functools
import math

import jax
import jax.numpy as jnp
from jax import lax
from jax.experimental import pallas as pl
from jax.experimental.pallas import tpu as pltpu

F32 = jnp.float32
BF16 = jnp.bfloat16

HEAD_DIM = 64
HEAD_WIDTH = 2 * HEAD_DIM
CONV_WIDTH = 31
SUBLANES = 8
HALO_ROWS = 32
NORM_EPS = 1e-6
LN_EPS = 1e-5
NEG_BIG = -1e30
Q_SCALE = math.log2(math.e) / math.sqrt(HEAD_DIM)
VMEM_LIMIT = 60 * 1024 * 1024

TM_IN = 512
TM_POST = 512
TQ = 512
CONV_ROWS = 64
CONV_LANES = 128
PROJ_COLS = 512


def _dot(a, b):
    return jnp.dot(a, b, preferred_element_type=F32)


def _dot_nt(a, b):
    return lax.dot_general(a, b, (((1,), (1,)), ((), ())), preferred_element_type=F32)


def _rms(x, g):
    y = x * lax.rsqrt(jnp.mean(jnp.square(x), axis=-1, keepdims=True) + NORM_EPS)
    return y * g


def _lam(lq, lam_init):
    a = jnp.sum(lq[0:1] * lq[1:2], axis=-1, keepdims=True)
    b = jnp.sum(lq[2:3] * lq[3:4], axis=-1, keepdims=True)
    return jnp.exp(a) - jnp.exp(b) + lam_init


def _const_spec(shape):
    n = len(shape)
    return pl.BlockSpec(shape, lambda *_: (0,) * n, pipeline_mode=pl.Buffered(1))


def _params(n_grid):
    return pltpu.CompilerParams(dimension_semantics=("arbitrary",) * n_grid,
                                vmem_limit_bytes=VMEM_LIMIT)


def _glu(h, w_ref, d_conv):
    zab = _dot(h, w_ref[:, 0:2 * d_conv])
    return zab[:, :d_conv] * jax.nn.sigmoid(zab[:, d_conv:])


def _store_q(q_ref, cs, z):
    q_ref[:, cs] = (z * Q_SCALE).astype(BF16)


def _store_kv(f_ref, b_ref, cs, z):
    if len(f_ref.shape) == 3:
        assert cs == slice(0, f_ref.shape[1] * f_ref.shape[2])
        f_ref[...] = z.reshape(f_ref.shape)
    else:
        f_ref[:, cs] = z
    if b_ref is not None:
        b_ref[:, cs] = z.astype(BF16)


def _store_gate(gate_ref, cs, z):
    gate_ref[:, cs] = jax.nn.sigmoid(z).astype(BF16)


def _qkvg_plan(w_ref, q_ref, k_ref, v_ref, kb_ref, vb_ref, gate_ref, *, d_conv, attn_w):
    off_q = 2 * d_conv
    off_k = off_q + attn_w
    off_v = off_k + attn_w
    off_g = off_v + attn_w
    kv_cols = attn_w if len(k_ref.shape) == 3 else PROJ_COLS
    return [(off_q, attn_w, PROJ_COLS, functools.partial(_store_q, q_ref)),
            (off_k, attn_w, kv_cols, functools.partial(_store_kv, k_ref, kb_ref)),
            (off_v, attn_w, kv_cols, functools.partial(_store_kv, v_ref, vb_ref)),
            (off_g, w_ref.shape[1] - off_g, PROJ_COLS, functools.partial(_store_gate, gate_ref))]


def _project(h_ref, w_ref, plan):
    for off, width, cols, store in plan:
        for c0 in range(0, width, cols):
            store(slice(c0, c0 + cols), _dot(h_ref[...], w_ref[:, off + c0:off + c0 + cols]))


def _conv_rows(ext_sc, wdw_ref, bdw_ref, uc_ref, r0, n_rows):
    lead = HALO_ROWS - (CONV_WIDTH - 1)
    win_rows = n_rows + HALO_ROWS
    r0 = pl.multiple_of(r0, n_rows)
    for c0 in range(0, ext_sc.shape[1], CONV_LANES):
        cols = slice(c0, c0 + CONV_LANES)
        acc = jnp.broadcast_to(bdw_ref[:, cols], (n_rows, CONV_LANES))
        win = ext_sc[pl.ds(r0, win_rows), cols]
        for b in range(SUBLANES):
            shifted = win if b == 0 else pltpu.roll(win, win_rows - b, axis=0)
            for a in range(HALO_ROWS // SUBLANES + 1):
                j = SUBLANES * a + b - lead
                if 0 <= j < CONV_WIDTH:
                    acc = acc + (shifted[SUBLANES * a:SUBLANES * a + n_rows, :]
                                 * wdw_ref[j:j + 1, cols])
        uc_ref[pl.ds(r0, n_rows), cols] = acc


def _inproj_prompt_kernel(x_ref, g_ref, w_ref, wdw_ref, bdw_ref, u_ref, uc_ref, q_ref, k_ref,
                          v_ref, kb_ref, vb_ref, gate_ref, ext_sc, h_sc, *, tm, tiles_per_seq,
                          d_conv, attn_w):
    @pl.when(pl.program_id(0) % tiles_per_seq == 0)
    def _():
        ext_sc[0:HALO_ROWS, :] = jnp.zeros((HALO_ROWS, d_conv), F32)

    h_sc[...] = _rms(x_ref[...], g_ref[...]).astype(BF16)
    u = _glu(h_sc[...], w_ref, d_conv)
    u_ref[...] = u
    ext_sc[HALO_ROWS:, :] = u

    def conv_trip(i, carry):
        _conv_rows(ext_sc, wdw_ref, bdw_ref, uc_ref, i * CONV_ROWS, CONV_ROWS)
        return carry
    lax.fori_loop(0, tm // CONV_ROWS, conv_trip, 0)
    ext_sc[0:HALO_ROWS, :] = ext_sc[tm:tm + HALO_ROWS, :]

    _project(h_sc, w_ref, _qkvg_plan(w_ref, q_ref, k_ref, v_ref, kb_ref, vb_ref, gate_ref,
                                     d_conv=d_conv, attn_w=attn_w))


def _inproj_sample_kernel(x_ref, g_ref, w_ref, u_ref, q_ref, k_ref, v_ref, gate_ref, h_sc,
                          *, d_conv, attn_w):
    h_sc[...] = _rms(x_ref[...], g_ref[...]).astype(BF16)
    u_ref[...] = _glu(h_sc[...], w_ref, d_conv)
    _project(h_sc, w_ref, _qkvg_plan(w_ref, q_ref, k_ref, v_ref, None, None, gate_ref,
                                     d_conv=d_conv, attn_w=attn_w))


def _without_refs(kern, first, count, *refs):
    return kern(*refs[:first], *refs[first + count:])


def _inproj_prompt(x, g, w, wdw, bdw, kv_all, *, layer, depth, tm, seq, d_conv, n_heads):
    t, d_model = x.shape
    attn_w = n_heads * HEAD_WIDTH
    row = lambda n: pl.BlockSpec((tm, n), lambda i: (i, 0))
    slab = pl.BlockSpec((None, tm, n_heads, HEAD_WIDTH), lambda i: (layer, i, 0, 0))
    sds = jax.ShapeDtypeStruct
    kern = functools.partial(_inproj_prompt_kernel, tm=tm, tiles_per_seq=seq // tm,
                             d_conv=d_conv, attn_w=attn_w)
    operands = [x, g, w, wdw, bdw]
    in_specs = [row(d_model), _const_spec(g.shape), _const_spec(w.shape),
                _const_spec(wdw.shape), _const_spec(bdw.shape)]
    aliases = {}
    if kv_all is not None:
        aliases = {len(operands): 3, len(operands) + 1: 4}
        kern = functools.partial(_without_refs, kern, len(operands), 2)
        operands += list(kv_all)
        in_specs += [pl.BlockSpec(memory_space=pl.ANY)] * 2
    return pl.pallas_call(
        kern,
        grid=(t // tm,),
        in_specs=in_specs,
        out_specs=[row(d_conv), row(d_conv), row(attn_w), slab, slab,
                   row(attn_w), row(attn_w), row(2 * d_model)],
        out_shape=[
            sds((t, d_conv), F32),
            sds((t, d_conv), F32),
            sds((t, attn_w), BF16),
            sds((depth, t, n_heads, HEAD_WIDTH), F32),
            sds((depth, t, n_heads, HEAD_WIDTH), F32),
            sds((t, attn_w), BF16),
            sds((t, attn_w), BF16),
            sds((t, 2 * d_model), BF16),
        ],
        input_output_aliases=aliases,
        scratch_shapes=[pltpu.VMEM((tm + HALO_ROWS, d_conv), F32),
                        pltpu.VMEM((tm, d_model), BF16)],
        compiler_params=_params(1),
    )(*operands)


def _inproj_sample(x, g, w, *, d_conv, attn_w):
    t, d_model = x.shape
    full = lambda n: pl.BlockSpec((t, n), lambda i: (0, 0))
    sds = jax.ShapeDtypeStruct
    kern = functools.partial(_inproj_sample_kernel, d_conv=d_conv, attn_w=attn_w)
    return pl.pallas_call(
        kern,
        grid=(1,),
        in_specs=[full(d_model), _const_spec(g.shape), _const_spec(w.shape)],
        out_specs=[full(d_conv), full(attn_w), full(attn_w), full(attn_w), full(2 * d_model)],
        out_shape=[sds((t, d_conv), F32), sds((t, attn_w), BF16), sds((t, attn_w), F32),
                   sds((t, attn_w), F32), sds((t, 2 * d_model), BF16)],
        scratch_shapes=[pltpu.VMEM((t, d_model), BF16)],
        compiler_params=_params(1),
    )(x, g, w)


def _prompt_attention_step(qi, lq_ref, sg_ref, q_ref, k_ref, o_ref, vext_sc, qs_sc, s0_sc, s1_sc,
                           m_sc, acc_sc, *, tq, lam_init):
    q = q_ref[0]
    lane = lax.broadcasted_iota(jnp.int32, q.shape, 1)
    zero = jnp.zeros_like(q)
    qs_sc[0:tq, :] = jnp.where(lane < HEAD_DIM, q, zero)
    qs_sc[tq:, :] = jnp.where(lane >= HEAD_DIM, q, zero)
    m_sc[...] = jnp.full(m_sc.shape, NEG_BIG, F32)
    acc_sc[...] = jnp.zeros(acc_sc.shape, F32)

    def scores(j, s_sc):
        start = pl.multiple_of(j * tq, tq)
        s_sc[...] = _dot_nt(qs_sc[...], k_ref[0, pl.ds(start, tq), :])

    def update(j, s_sc, masked):
        s = s_sc[...]
        if masked:
            r = lax.broadcasted_iota(jnp.int32, s.shape, 0)
            c = lax.broadcasted_iota(jnp.int32, s.shape, 1)
            s = jnp.where(c <= jnp.where(r >= tq, r - tq, r), s, NEG_BIG)
        m = m_sc[...]
        m_new = jnp.maximum(m, jnp.max(s, axis=-1, keepdims=True))
        m_sc[...] = m_new
        alpha = jnp.exp2(m - m_new)
        reps = lambda a, n: jnp.concatenate([a] * (n // HEAD_WIDTH), axis=1)
        p = jnp.exp2((s - reps(m_new, tq)).astype(BF16))
        start = pl.multiple_of(j * tq, tq)
        acc_sc[...] = (reps(alpha, 2 * HEAD_WIDTH) * acc_sc[...]
                       + _dot(p, vext_sc[pl.ds(start, tq), :]))

    scores(0, s0_sc)

    def pair(jj, carry):
        j = 2 * jj
        scores(j + 1, s1_sc)
        update(j, s0_sc, False)
        scores(j + 2, s0_sc)
        update(j + 1, s1_sc, False)
        return carry

    lax.fori_loop(0, qi // 2, pair, 0)

    @pl.when(qi % 2 == 0)
    def _():
        update(qi, s0_sc, True)

    @pl.when(qi % 2 == 1)
    def _():
        scores(qi, s1_sc)
        update(qi - 1, s0_sc, False)
        update(qi, s1_sc, True)

    lam = _lam(lq_ref[...], lam_init)
    acc = acc_sc[...]
    a = acc[:, :HEAD_WIDTH] / acc[:, HEAD_WIDTH:]
    o = a[:tq] - lam * a[tq:]
    o = _rms(o, sg_ref[...]) * (1.0 - lam_init)
    o_ref[0] = o.astype(o_ref.dtype)


def _stacked_q_rows(q_ref):
    q = q_ref[0].astype(F32)
    lane = lax.broadcasted_iota(jnp.int32, q.shape, 1)
    return jnp.concatenate([jnp.where(lane < HEAD_DIM, q, 0.0),
                            jnp.where(lane >= HEAD_DIM, q, 0.0)], axis=0)


def _decode_attention_init(q_ref, kn_ref, vn_ref, m_sc, l_sc, acc_sc):
    kn = kn_ref[0]
    vn = vn_ref[0]
    m_sc[...] = jnp.sum(_stacked_q_rows(q_ref) * jnp.concatenate([kn, kn], axis=0),
                        axis=-1, keepdims=True)
    l_sc[...] = jnp.ones_like(l_sc)
    acc_sc[...] = jnp.concatenate([vn, vn], axis=0)


def _decode_attention_step(lq_ref, sg_ref, q_ref, k_refs, v_refs, o_ref, m_sc, l_sc, acc_sc,
                           *, n_heads, lam_init):
    page_rows = k_refs[0].shape[0] * n_heads
    qrows = _stacked_q_rows(q_ref).astype(BF16)

    s = jnp.concatenate(
        [_dot_nt(qrows, kr[...].reshape(page_rows, HEAD_WIDTH).astype(BF16)) for kr in k_refs],
        axis=1)
    r = lax.broadcasted_iota(jnp.int32, s.shape, 0)
    c = lax.broadcasted_iota(jnp.int32, s.shape, 1)
    s = jnp.where((c % n_heads) == (r % n_heads), s, NEG_BIG)
    m = m_sc[...]
    m_new = jnp.maximum(m, jnp.max(s, axis=-1, keepdims=True))
    alpha = jnp.exp2(m - m_new)
    p = jnp.exp2(s - m_new)
    l = alpha * l_sc[...] + jnp.sum(p, axis=-1, keepdims=True)
    l_sc[...] = l
    m_sc[...] = m_new
    pb = p.astype(BF16)
    pv = None
    for i, vr in enumerate(v_refs):
        part = _dot(pb[:, i * page_rows:(i + 1) * page_rows],
                    vr[...].reshape(page_rows, HEAD_WIDTH).astype(BF16))
        pv = part if pv is None else pv + part
    acc = alpha * acc_sc[...] + pv
    acc_sc[...] = acc

    a = acc / l
    o = a[:n_heads] - _lam(lq_ref[...], lam_init) * a[n_heads:]
    o = _rms(o, sg_ref[...]) * (1.0 - lam_init)
    o_ref[0] = o.astype(o_ref.dtype)


def _attention_kernel(pt_ref, lq_ref, sg_ref, q_ref, k_ref, v_ref, qd_ref, kn_ref, vn_ref, *rest,
                      n_heads, n_pages_step, steps_per_seq, tq, lam_init):
    del pt_ref
    k_pages = rest[:n_pages_step]
    v_pages = rest[n_pages_step:2 * n_pages_step]
    (o_ref, od_ref, vext_sc, qs_sc, s0_sc, s1_sc, m_sc, acc_sc,
     md_sc, ld_sc, accd_sc) = rest[2 * n_pages_step:]
    qi = pl.program_id(2)
    step = (pl.program_id(0) * pl.num_programs(1) + pl.program_id(1)) * pl.num_programs(2) + qi

    @pl.when(step % steps_per_seq == 0)
    def _():
        _decode_attention_init(qd_ref, kn_ref, vn_ref, md_sc, ld_sc, accd_sc)

    @pl.when(qi == 0)
    def _():
        vext_sc[:, 0:HEAD_WIDTH] = v_ref[0]
        vext_sc[:, HEAD_WIDTH:] = jnp.ones((vext_sc.shape[0], HEAD_WIDTH), BF16)

    _decode_attention_step(lq_ref, sg_ref, qd_ref, k_pages, v_pages, od_ref, md_sc, ld_sc, accd_sc,
                           n_heads=n_heads, lam_init=lam_init)
    _prompt_attention_step(qi, lq_ref, sg_ref, q_ref, k_ref, o_ref, vext_sc, qs_sc, s0_sc, s1_sc,
                           m_sc, acc_sc, tq=tq, lam_init=lam_init)


def _attention(page_table, lq, sg, q, kb, vb, qd, kn, vn, cache_k, cache_v, *, layer, tq,
               lam_init):
    b, s, _ = q.shape
    db = qd.shape[0]
    _, _, page_size, n_heads, head_w = cache_k.shape
    n_pages = page_table.shape[1]
    n_steps = b * n_heads * (s // tq)
    pages_step = db * n_pages // n_steps
    assert pages_step * n_steps == db * n_pages and n_pages % pages_step == 0
    steps_per_seq = n_pages // pages_step

    def step_of(bi, h, i):
        return (bi * n_heads + h) * (s // tq) + i

    const = lambda a: pl.BlockSpec(a.shape, lambda bi, h, i, pt: (0,) * a.ndim)
    tok = pl.BlockSpec((1, n_heads, head_w),
                       lambda bi, h, i, pt: (step_of(bi, h, i) // steps_per_seq, 0, 0))
    head_rows = lambda rows, blk: pl.BlockSpec((1, rows, HEAD_WIDTH),
                                               lambda bi, h, i, pt: (bi, blk(i), h))

    def page_spec(n):
        def index(bi, h, i, pt):
            t = step_of(bi, h, i)
            return (layer, pt[t // steps_per_seq, (t % steps_per_seq) * pages_step + n], 0, 0, 0)
        return pl.BlockSpec((None, None, page_size, n_heads, head_w), index)

    kern = functools.partial(_attention_kernel, n_heads=n_heads, n_pages_step=pages_step,
                             steps_per_seq=steps_per_seq, tq=tq, lam_init=lam_init)
    rows = 2 * n_heads
    return pl.pallas_call(
        kern,
        grid_spec=pltpu.PrefetchScalarGridSpec(
            num_scalar_prefetch=1,
            grid=(b, n_heads, s // tq),
            in_specs=[const(lq), const(sg),
                      head_rows(tq, lambda i: i), head_rows(s, lambda i: 0),
                      head_rows(s, lambda i: 0), tok, tok, tok]
                     + [page_spec(n) for n in range(pages_step)]
                     + [page_spec(n) for n in range(pages_step)],
            out_specs=[head_rows(tq, lambda i: i), tok],
            scratch_shapes=[pltpu.VMEM((s, 2 * HEAD_WIDTH), BF16),
                            pltpu.VMEM((2 * tq, HEAD_WIDTH), BF16),
                            pltpu.VMEM((2 * tq, tq), F32),
                            pltpu.VMEM((2 * tq, tq), F32),
                            pltpu.VMEM((2 * tq, HEAD_WIDTH), F32),
                            pltpu.VMEM((2 * tq, 2 * HEAD_WIDTH), F32),
                            pltpu.VMEM((rows, 1), F32),
                            pltpu.VMEM((rows, 1), F32),
                            pltpu.VMEM((rows, head_w), F32)],
        ),
        out_shape=[jax.ShapeDtypeStruct(q.shape, BF16),
                   jax.ShapeDtypeStruct((db, n_heads, head_w), BF16)],
        compiler_params=_params(3),
    )(page_table, lq, sg, q, kb, vb, qd, kn, vn,
      *([cache_k] * pages_step), *([cache_v] * pages_step))


def _mix_and_mlp(x, uc, o, gate, lng, lnb, wco, wao, wo, wup, wdn, n_post, n_pre, n_mpost,
                 *, d_model, ff_chunk):
    mu = jnp.mean(uc, axis=-1, keepdims=True)
    dc = uc - mu
    var = jnp.mean(jnp.square(dc), axis=-1, keepdims=True)
    ln = dc * lax.rsqrt(var + LN_EPS) * lng[...] + lnb[...]
    y_conv = _dot((ln * jax.nn.sigmoid(ln)).astype(BF16), wco[...])
    y_attn = _dot(o, wao[...])
    mix = gate[:, :d_model].astype(F32) * y_conv + gate[:, d_model:].astype(F32) * y_attn
    x1 = x + _rms(_dot(mix.astype(BF16), wo[...]), n_post[...])
    h2 = _rms(x1, n_pre[...]).astype(BF16)
    d_ff = wup.shape[1]
    f = None
    for c0 in range(0, d_ff, ff_chunk):
        a = jnp.square(jnp.maximum(_dot(h2, wup[:, c0:c0 + ff_chunk]), 0.0)).astype(BF16)
        part = _dot(a, wdn[c0:c0 + ff_chunk, :])
        f = part if f is None else f + part
    return x1 + _rms(f, n_mpost[...])


def _post_prompt_kernel(x_ref, uc_ref, o_ref, gate_ref, lng, lnb, wco, wao, wo, wup, wdn,
                        n_post, n_pre, n_mpost, out_ref, *, d_model, ff_chunk):
    out_ref[...] = _mix_and_mlp(x_ref[...], uc_ref[...], o_ref[...], gate_ref[...], lng, lnb,
                                wco, wao, wo, wup, wdn, n_post, n_pre, n_mpost,
                                d_model=d_model, ff_chunk=ff_chunk)


def _post_sample_kernel(x_ref, u_ref, st_ref, o_ref, gate_ref, wdw_ref, bdw_ref, lng, lnb,
                        wco, wao, wo, wup, wdn, n_post, n_pre, n_mpost, out_ref,
                        *, d_model, ff_chunk):
    uc = bdw_ref[...] + u_ref[...] * wdw_ref[CONV_WIDTH - 1:CONV_WIDTH, :]
    for j in range(CONV_WIDTH - 1):
        uc = uc + st_ref[j] * wdw_ref[j:j + 1, :]
    out_ref[...] = _mix_and_mlp(x_ref[...], uc, o_ref[...], gate_ref[...], lng, lnb,
                                wco, wao, wo, wup, wdn, n_post, n_pre, n_mpost,
                                d_model=d_model, ff_chunk=ff_chunk)


def _weight_specs(ws):
    return [_const_spec(w.shape) for w in ws]


def _post_prompt(x, uc, o, gate, weights, *, tm, ff_chunk):
    t, d_model = x.shape
    row = lambda n: pl.BlockSpec((tm, n), lambda i: (i, 0))
    kern = functools.partial(_post_prompt_kernel, d_model=d_model, ff_chunk=ff_chunk)
    return pl.pallas_call(
        kern,
        grid=(t // tm,),
        in_specs=[row(d_model), row(uc.shape[1]), row(o.shape[1]), row(2 * d_model)]
                 + _weight_specs(weights),
        out_specs=row(d_model),
        out_shape=jax.ShapeDtypeStruct((t, d_model), F32),
        compiler_params=_params(1),
    )(x, uc, o, gate, *weights)


def _post_sample(x, u, state_t, o, gate, conv_w, weights, *, ff_chunk):
    t, d_model = x.shape
    full = lambda a: pl.BlockSpec(a.shape, lambda i: (0,) * a.ndim)
    kern = functools.partial(_post_sample_kernel, d_model=d_model, ff_chunk=ff_chunk)
    return pl.pallas_call(
        kern,
        grid=(1,),
        in_specs=[full(x), full(u), full(state_t), full(o), full(gate)]
                 + _weight_specs(conv_w + weights),
        out_specs=pl.BlockSpec((t, d_model), lambda i: (0, 0)),
        out_shape=jax.ShapeDtypeStruct((t, d_model), F32),
        compiler_params=_params(1),
    )(x, u, state_t, o, gate, *conv_w, *weights)


def kernel(x_prompt, x_sample, cache_k, cache_v, state_conv, page_table, norm_mix_pre, norm_mix_post, norm_mlp_pre, norm_mlp_post, w_in, w_dw, b_dw, conv_ln_g, conv_ln_b, w_conv_out, lambda_qk, subln_g, w_attn_out, w_out, w_up, w_down):
    batch, seq, d_model = x_prompt.shape
    dec_batch, dec_seq, _ = x_sample.shape
    assert dec_seq == 1
    depth, _, _, n_heads, head_w = cache_k.shape
    assert head_w == HEAD_WIDTH and state_conv.shape[2] == CONV_WIDTH - 1
    d_conv = state_conv.shape[3]
    attn_w = n_heads * HEAD_WIDTH
    ff_chunk = d_model

    xp = x_prompt.reshape(batch * seq, d_model)
    xs = x_sample.reshape(dec_batch, d_model)
    row = lambda a: a.reshape(1, -1)

    cp, ks, vs, cs = [], [], [], []
    kv_all = None
    for l in range(depth):
        lam_init = 0.8 - 0.6 * math.exp(-0.3 * l)
        w_in_b = w_in[l].astype(BF16)
        conv_w = (w_dw[l], row(b_dw[l]))
        weights = (row(conv_ln_g[l]), row(conv_ln_b[l]),
                   w_conv_out[l].astype(BF16), w_attn_out[l].astype(BF16),
                   w_out[l].astype(BF16), w_up[l].astype(BF16), w_down[l].astype(BF16),
                   row(norm_mix_post[l]), row(norm_mlp_pre[l]), row(norm_mlp_post[l]))
        g_pre = row(norm_mix_pre[l])
        lq = lambda_qk[l]
        sg = row(subln_g[l])

        u, uc, q, k_all, v_all, kb, vb, gate = _inproj_prompt(
            xp, g_pre, w_in_b, *conv_w, kv_all, layer=l, depth=depth, tm=TM_IN, seq=seq,
            d_conv=d_conv, n_heads=n_heads)
        kv_all = (k_all, v_all)
        us, qs, k_s, v_s, gate_s = _inproj_sample(xs, g_pre, w_in_b, d_conv=d_conv,
                                                  attn_w=attn_w)
        seqs = lambda a: a.reshape(batch, seq, attn_w)
        heads = lambda a: a.reshape(dec_batch, n_heads, HEAD_WIDTH)
        o, o_s = _attention(page_table, lq, sg, seqs(q), seqs(kb), seqs(vb), heads(qs),
                            heads(k_s), heads(v_s), cache_k, cache_v, layer=l, tq=TQ,
                            lam_init=lam_init)

        xp = _post_prompt(xp, uc, o.reshape(batch * seq, attn_w), gate, weights, tm=TM_POST,
                          ff_chunk=ff_chunk)
        cp.append(u.reshape(batch, seq, d_conv)[:, seq - (CONV_WIDTH - 1):])

        st = state_conv[l]
        xs = _post_sample(xs, us, st.transpose(1, 0, 2), o_s.reshape(dec_batch, attn_w), gate_s,
                          conv_w, weights, ff_chunk=ff_chunk)
        ks.append(k_s.reshape(dec_batch, 1, n_heads, HEAD_WIDTH))
        vs.append(v_s.reshape(dec_batch, 1, n_heads, HEAD_WIDTH))
        cs.append(jnp.concatenate([st[:, 1:], us[:, None, :]], axis=1))

    k_prompt, v_prompt = (a.reshape(depth, batch, seq, n_heads, HEAD_WIDTH) for a in kv_all)
    return (xp.reshape(batch, seq, d_model), xs.reshape(dec_batch, 1, d_model),
            k_prompt, v_prompt, jnp.stack(cp),
            jnp.stack(ks), jnp.stack(vs), jnp.stack(cs))
```

```python
import functools
import math

import jax
import jax.numpy as jnp
from jax import lax
from jax.experimental import pallas as pl
from jax.experimental.pallas import tpu as pltpu

F32 = jnp.float32
BF16 = jnp.bfloat16

HEAD_DIM = 64
HEAD_WIDTH = 2 * HEAD_DIM
CONV_WIDTH = 31
SUBLANES = 8
HALO_ROWS = 32
NORM_EPS = 1e-6
LN_EPS = 1e-5
NEG_BIG = -1e30
Q_SCALE = math.log2(math.e) / math.sqrt(HEAD_DIM)
VMEM_LIMIT = 60 * 1024 * 1024

TM_IN = 512
TM_POST = 512
TQ = 512
CONV_ROWS = 64
CONV_LANES = 128
PROJ_COLS = 512


def _dot(a, b):
    return jnp.dot(a, b, preferred_element_type=F32)


def _dot_nt(a, b):
    return lax.dot_general(a, b, (((1,), (1,)), ((), ())), preferred_element_type=F32)


def _rms(x, g):
    y = x * lax.rsqrt(jnp.mean(jnp.square(x), axis=-1, keepdims=True) + NORM_EPS)
    return y * g


def _lam(lq, lam_init):
    a = jnp.sum(lq[0:1] * lq[1:2], axis=-1, keepdims=True)
    b = jnp.sum(lq[2:3] * lq[3:4], axis=-1, keepdims=True)
    return jnp.exp(a) - jnp.exp(b) + lam_init


def _const_spec(shape):
    n = len(shape)
    return pl.BlockSpec(shape, lambda *_: (0,) * n, pipeline_mode=pl.Buffered(1))


def _params(n_grid):
    return pltpu.CompilerParams(dimension_semantics=("arbitrary",) * n_grid,
                                vmem_limit_bytes=VMEM_LIMIT)


def _glu(h, w_ref, d_conv):
    zab = _dot(h, w_ref[:, 0:2 * d_conv])
    return zab[:, :d_conv] * jax.nn.sigmoid(zab[:, d_conv:])


def _store_q(q_ref, cs, z):
    q_ref[:, cs] = (z * Q_SCALE).astype(BF16)


def _store_kv(f_ref, b_ref, cs, z):
    if len(f_ref.shape) == 3:
        assert cs == slice(0, f_ref.shape[1] * f_ref.shape[2])
        f_ref[...] = z.reshape(f_ref.shape)
    else:
        f_ref[:, cs] = z
    if b_ref is not None:
        b_ref[:, cs] = z.astype(BF16)


def _store_gate(gate_ref, cs, z):
    gate_ref[:, cs] = jax.nn.sigmoid(z).astype(BF16)


def _qkvg_plan(w_ref, q_ref, k_ref, v_ref, kb_ref, vb_ref, gate_ref, *, d_conv, attn_w):
    off_q = 2 * d_conv
    off_k = off_q + attn_w
    off_v = off_k + attn_w
    off_g = off_v + attn_w
    kv_cols = attn_w if len(k_ref.shape) == 3 else PROJ_COLS
    return [(off_q, attn_w, PROJ_COLS, functools.partial(_store_q, q_ref)),
            (off_k, attn_w, kv_cols, functools.partial(_store_kv, k_ref, kb_ref)),
            (off_v, attn_w, kv_cols, functools.partial(_store_kv, v_ref, vb_ref)),
            (off_g, w_ref.shape[1] - off_g, PROJ_COLS, functools.partial(_store_gate, gate_ref))]


def _project(h_ref, w_ref, plan):
    for off, width, cols, store in plan:
        for c0 in range(0, width, cols):
            store(slice(c0, c0 + cols), _dot(h_ref[...], w_ref[:, off + c0:off + c0 + cols]))


def _conv_rows(ext_sc, wdw_ref, bdw_ref, uc_ref, r0, n_rows):
    lead = HALO_ROWS - (CONV_WIDTH - 1)
    win_rows = n_rows + HALO_ROWS
    r0 = pl.multiple_of(r0, n_rows)
    for c0 in range(0, ext_sc.shape[1], CONV_LANES):
        cols = slice(c0, c0 + CONV_LANES)
        acc = jnp.broadcast_to(bdw_ref[:, cols], (n_rows, CONV_LANES))
        win = ext_sc[pl.ds(r0, win_rows), cols]
        for b in range(SUBLANES):
            shifted = win if b == 0 else pltpu.roll(win, win_rows - b, axis=0)
            for a in range(HALO_ROWS // SUBLANES + 1):
                j = SUBLANES * a + b - lead
                if 0 <= j < CONV_WIDTH:
                    acc = acc + (shifted[SUBLANES * a:SUBLANES * a + n_rows, :]
                                 * wdw_ref[j:j + 1, cols])
        uc_ref[pl.ds(r0, n_rows), cols] = acc


def _inproj_prompt_kernel(x_ref, g_ref, w_ref, wdw_ref, bdw_ref, u_ref, uc_ref, q_ref, k_ref,
                          v_ref, kb_ref, vb_ref, gate_ref, ext_sc, h_sc, *, tm, tiles_per_seq,
                          d_conv, attn_w):
    @pl.when(pl.program_id(0) % tiles_per_seq == 0)
    def _():
        ext_sc[0:HALO_ROWS, :] = jnp.zeros((HALO_ROWS, d_conv), F32)

    h_sc[...] = _rms(x_ref[...], g_ref[...]).astype(BF16)
    u = _glu(h_sc[...], w_ref, d_conv)
    u_ref[...] = u
    ext_sc[HALO_ROWS:, :] = u

    def conv_trip(i, carry):
        _conv_rows(ext_sc, wdw_ref, bdw_ref, uc_ref, i * CONV_ROWS, CONV_ROWS)
        return carry
    lax.fori_loop(0, tm // CONV_ROWS, conv_trip, 0)
    ext_sc[0:HALO_ROWS, :] = ext_sc[tm:tm + HALO_ROWS, :]

    _project(h_sc, w_ref, _qkvg_plan(w_ref, q_ref, k_ref, v_ref, kb_ref, vb_ref, gate_ref,
                                     d_conv=d_conv, attn_w=attn_w))


def _inproj_sample_kernel(x_ref, g_ref, w_ref, u_ref, q_ref, k_ref, v_ref, gate_ref, h_sc,
                          *, d_conv, attn_w):
    h_sc[...] = _rms(x_ref[...], g_ref[...]).astype(BF16)
    u_ref[...] = _glu(h_sc[...], w_ref, d_conv)
    _project(h_sc, w_ref, _qkvg_plan(w_ref, q_ref, k_ref, v_ref, None, None, gate_ref,
                                     d_conv=d_conv, attn_w=attn_w))


def _without_refs(kern, first, count, *refs):
    return kern(*refs[:first], *refs[first + count:])


def _inproj_prompt(x, g, w, wdw, bdw, kv_all, *, layer, depth, tm, seq, d_conv, n_heads):
    t, d_model = x.shape
    attn_w = n_heads * HEAD_WIDTH
    row = lambda n: pl.BlockSpec((tm, n), lambda i: (i, 0))
    slab = pl.BlockSpec((None, tm, n_heads, HEAD_WIDTH), lambda i: (layer, i, 0, 0))
    sds = jax.ShapeDtypeStruct
    kern = functools.partial(_inproj_prompt_kernel, tm=tm, tiles_per_seq=seq // tm,
                             d_conv=d_conv, attn_w=attn_w)
    operands = [x, g, w, wdw, bdw]
    in_specs = [row(d_model), _const_spec(g.shape), _const_spec(w.shape),
                _const_spec(wdw.shape), _const_spec(bdw.shape)]
    aliases = {}
    if kv_all is not None:
        aliases = {len(operands): 3, len(operands) + 1: 4}
        kern = functools.partial(_without_refs, kern, len(operands), 2)
        operands += list(kv_all)
        in_specs += [pl.BlockSpec(memory_space=pl.ANY)] * 2
    return pl.pallas_call(
        kern,
        grid=(t // tm,),
        in_specs=in_specs,
        out_specs=[row(d_conv), row(d_conv), row(attn_w), slab, slab,
                   row(attn_w), row(attn_w), row(2 * d_model)],
        out_shape=[
            sds((t, d_conv), F32),
            sds((t, d_conv), F32),
            sds((t, attn_w), BF16),
            sds((depth, t, n_heads, HEAD_WIDTH), F32),
            sds((depth, t, n_heads, HEAD_WIDTH), F32),
            sds((t, attn_w), BF16),
            sds((t, attn_w), BF16),
            sds((t, 2 * d_model), BF16),
        ],
        input_output_aliases=aliases,
        scratch_shapes=[pltpu.VMEM((tm + HALO_ROWS, d_conv), F32),
                        pltpu.VMEM((tm, d_model), BF16)],
        compiler_params=_params(1),
    )(*operands)


def _inproj_sample(x, g, w, *, d_conv, attn_w):
    t, d_model = x.shape
    full = lambda n: pl.BlockSpec((t, n), lambda i: (0, 0))
    sds = jax.ShapeDtypeStruct
    kern = functools.partial(_inproj_sample_kernel, d_conv=d_conv, attn_w=attn_w)
    return pl.pallas_call(
        kern,
        grid=(1,),
        in_specs=[full(d_model), _const_spec(g.shape), _const_spec(w.shape)],
        out_specs=[full(d_conv), full(attn_w), full(attn_w), full(attn_w), full(2 * d_model)],
        out_shape=[sds((t, d_conv), F32), sds((t, attn_w), BF16), sds((t, attn_w), F32),
                   sds((t, attn_w), F32), sds((t, 2 * d_model), BF16)],
        scratch_shapes=[pltpu.VMEM((t, d_model), BF16)],
        compiler_params=_params(1),
    )(x, g, w)


def _prompt_attention_step(qi, lq_ref, sg_ref, q_ref, k_ref, o_ref, vext_sc, qs_sc, s0_sc, s1_sc,
                           m_sc, acc_sc, *, tq, lam_init, side_work):
    q = q_ref[0]
    lane = lax.broadcasted_iota(jnp.int32, q.shape, 1)
    zero = jnp.zeros_like(q)
    qs_sc[0:tq, :] = jnp.where(lane < HEAD_DIM, q, zero)
    qs_sc[tq:, :] = jnp.where(lane >= HEAD_DIM, q, zero)
    m_sc[...] = jnp.full(m_sc.shape, NEG_BIG, F32)
    acc_sc[...] = jnp.zeros(acc_sc.shape, F32)

    def scores(j, s_sc):
        start = pl.multiple_of(j * tq, tq)
        s_sc[...] = _dot_nt(qs_sc[...], k_ref[0, pl.ds(start, tq), :])

    def update(j, s_sc, masked):
        s = s_sc[...]
        if masked:
            r = lax.broadcasted_iota(jnp.int32, s.shape, 0)
            c = lax.broadcasted_iota(jnp.int32, s.shape, 1)
            s = jnp.where(c <= jnp.where(r >= tq, r - tq, r), s, NEG_BIG)
        m = m_sc[...]
        m_new = jnp.maximum(m, jnp.max(s, axis=-1, keepdims=True))
        m_sc[...] = m_new
        alpha = jnp.exp2(m - m_new)
        reps = lambda a, n: jnp.concatenate([a] * (n // HEAD_WIDTH), axis=1)
        p = jnp.exp2((s - reps(m_new, tq)).astype(BF16))
        start = pl.multiple_of(j * tq, tq)
        acc_sc[...] = (reps(alpha, 2 * HEAD_WIDTH) * acc_sc[...]
                       + _dot(p, vext_sc[pl.ds(start, tq), :]))

    scores(0, s0_sc)

    def pair(jj, carry):
        j = 2 * jj
        scores(j + 1, s1_sc)
        update(j, s0_sc, False)
        scores(j + 2, s0_sc)
        update(j + 1, s1_sc, False)
        return carry

    lax.fori_loop(0, qi // 2, pair, 0)

    @pl.when(qi % 2 == 0)
    def _():
        side_work()
        update(qi, s0_sc, True)

    @pl.when(qi % 2 == 1)
    def _():
        scores(qi, s1_sc)
        side_work()
        update(qi - 1, s0_sc, False)
        update(qi, s1_sc, True)

    lam = _lam(lq_ref[...], lam_init)
    acc = acc_sc[...]
    a = acc[:, :HEAD_WIDTH] / acc[:, HEAD_WIDTH:]
    o = a[:tq] - lam * a[tq:]
    o = _rms(o, sg_ref[...]) * (1.0 - lam_init)
    o_ref[0] = o.astype(o_ref.dtype)


def _stacked_q_rows(q_ref):
    q = q_ref[0].astype(F32)
    lane = lax.broadcasted_iota(jnp.int32, q.shape, 1)
    return jnp.concatenate([jnp.where(lane < HEAD_DIM, q, 0.0),
                            jnp.where(lane >= HEAD_DIM, q, 0.0)], axis=0)


def _decode_attention_init(q_ref, kn_ref, vn_ref, m_sc, l_sc, acc_sc):
    kn = kn_ref[0]
    vn = vn_ref[0]
    m_sc[...] = jnp.sum(_stacked_q_rows(q_ref) * jnp.concatenate([kn, kn], axis=0),
                        axis=-1, keepdims=True)
    l_sc[...] = jnp.ones_like(l_sc)
    acc_sc[...] = jnp.concatenate([vn, vn], axis=0)


def _decode_attention_step(lq_ref, sg_ref, q_ref, k_refs, v_refs, o_ref, m_sc, l_sc, acc_sc,
                           *, n_heads, lam_init):
    page_rows = k_refs[0].shape[0] * n_heads
    qrows = _stacked_q_rows(q_ref).astype(BF16)

    s = jnp.concatenate(
        [_dot_nt(qrows, kr[...].reshape(page_rows, HEAD_WIDTH).astype(BF16)) for kr in k_refs],
        axis=1)
    r = lax.broadcasted_iota(jnp.int32, s.shape, 0)
    c = lax.broadcasted_iota(jnp.int32, s.shape, 1)
    s = jnp.where((c % n_heads) == (r % n_heads), s, NEG_BIG)
    m = m_sc[...]
    m_new = jnp.maximum(m, jnp.max(s, axis=-1, keepdims=True))
    alpha = jnp.exp2(m - m_new)
    p = jnp.exp2(s - m_new)
    l = alpha * l_sc[...] + jnp.sum(p, axis=-1, keepdims=True)
    l_sc[...] = l
    m_sc[...] = m_new
    pb = p.astype(BF16)
    pv = None
    for i, vr in enumerate(v_refs):
        part = _dot(pb[:, i * page_rows:(i + 1) * page_rows],
                    vr[...].reshape(page_rows, HEAD_WIDTH).astype(BF16))
        pv = part if pv is None else pv + part
    acc = alpha * acc_sc[...] + pv
    acc_sc[...] = acc

    a = acc / l
    o = a[:n_heads] - _lam(lq_ref[...], lam_init) * a[n_heads:]
    o = _rms(o, sg_ref[...]) * (1.0 - lam_init)
    o_ref[0] = o.astype(o_ref.dtype)


def _attention_kernel(pt_ref, lq_ref, sg_ref, q_ref, k_ref, v_ref, qd_ref, kn_ref, vn_ref, *rest,
                      n_heads, n_pages_step, steps_per_seq, tq, lam_init):
    del pt_ref
    k_pages = rest[:n_pages_step]
    v_pages = rest[n_pages_step:2 * n_pages_step]
    (o_ref, od_ref, vext_sc, qs_sc, s0_sc, s1_sc, m_sc, acc_sc,
     md_sc, ld_sc, accd_sc) = rest[2 * n_pages_step:]
    qi = pl.program_id(2)
    step = (pl.program_id(0) * pl.num_programs(1) + pl.program_id(1)) * pl.num_programs(2) + qi

    @pl.when(step % steps_per_seq == 0)
    def _():
        _decode_attention_init(qd_ref, kn_ref, vn_ref, md_sc, ld_sc, accd_sc)

    @pl.when(qi == 0)
    def _():
        vext_sc[:, 0:HEAD_WIDTH] = v_ref[0]
        vext_sc[:, HEAD_WIDTH:] = jnp.ones((vext_sc.shape[0], HEAD_WIDTH), BF16)

    decode_step = functools.partial(
        _decode_attention_step, lq_ref, sg_ref, qd_ref, k_pages, v_pages, od_ref, md_sc, ld_sc,
        accd_sc, n_heads=n_heads, lam_init=lam_init)
    _prompt_attention_step(qi, lq_ref, sg_ref, q_ref, k_ref, o_ref, vext_sc, qs_sc, s0_sc, s1_sc,
                           m_sc, acc_sc, tq=tq, lam_init=lam_init, side_work=decode_step)


def _attention(page_table, lq, sg, q, kb, vb, qd, kn, vn, cache_k, cache_v, *, layer, tq,
               lam_init):
    b, s, _ = q.shape
    db = qd.shape[0]
    _, _, page_size, n_heads, head_w = cache_k.shape
    n_pages = page_table.shape[1]
    n_steps = b * n_heads * (s // tq)
    pages_step = db * n_pages // n_steps
    assert pages_step * n_steps == db * n_pages and n_pages % pages_step == 0
    steps_per_seq = n_pages // pages_step

    def step_of(bi, h, i):
        return (bi * n_heads + h) * (s // tq) + i

    const = lambda a: pl.BlockSpec(a.shape, lambda bi, h, i, pt: (0,) * a.ndim)
    tok = pl.BlockSpec((1, n_heads, head_w),
                       lambda bi, h, i, pt: (step_of(bi, h, i) // steps_per_seq, 0, 0))
    head_rows = lambda rows, blk: pl.BlockSpec((1, rows, HEAD_WIDTH),
                                               lambda bi, h, i, pt: (bi, blk(i), h))

    def page_spec(n):
        def index(bi, h, i, pt):
            t = step_of(bi, h, i)
            return (layer, pt[t // steps_per_seq, (t % steps_per_seq) * pages_step + n], 0, 0, 0)
        return pl.BlockSpec((None, None, page_size, n_heads, head_w), index)

    kern = functools.partial(_attention_kernel, n_heads=n_heads, n_pages_step=pages_step,
                             steps_per_seq=steps_per_seq, tq=tq, lam_init=lam_init)
    rows = 2 * n_heads
    return pl.pallas_call(
        kern,
        grid_spec=pltpu.PrefetchScalarGridSpec(
            num_scalar_prefetch=1,
            grid=(b, n_heads, s // tq),
            in_specs=[const(lq), const(sg),
                      head_rows(tq, lambda i: i), head_rows(s, lambda i: 0),
                      head_rows(s, lambda i: 0), tok, tok, tok]
                     + [page_spec(n) for n in range(pages_step)]
                     + [page_spec(n) for n in range(pages_step)],
            out_specs=[head_rows(tq, lambda i: i), tok],
            scratch_shapes=[pltpu.VMEM((s, 2 * HEAD_WIDTH), BF16),
                            pltpu.VMEM((2 * tq, HEAD_WIDTH), BF16),
                            pltpu.VMEM((2 * tq, tq), F32),
                            pltpu.VMEM((2 * tq, tq), F32),
                            pltpu.VMEM((2 * tq, HEAD_WIDTH), F32),
                            pltpu.VMEM((2 * tq, 2 * HEAD_WIDTH), F32),
                            pltpu.VMEM((rows, 1), F32),
                            pltpu.VMEM((rows, 1), F32),
                            pltpu.VMEM((rows, head_w), F32)],
        ),
        out_shape=[jax.ShapeDtypeStruct(q.shape, BF16),
                   jax.ShapeDtypeStruct((db, n_heads, head_w), BF16)],
        compiler_params=_params(3),
    )(page_table, lq, sg, q, kb, vb, qd, kn, vn,
      *([cache_k] * pages_step), *([cache_v] * pages_step))


def _mix_and_mlp(x, uc, o, gate, lng, lnb, wco, wao, wo, wup, wdn, n_post, n_pre, n_mpost,
                 *, d_model, ff_chunk):
    mu = jnp.mean(uc, axis=-1, keepdims=True)
    dc = uc - mu
    var = jnp.mean(jnp.square(dc), axis=-1, keepdims=True)
    ln = dc * lax.rsqrt(var + LN_EPS) * lng[...] + lnb[...]
    y_conv = _dot((ln * jax.nn.sigmoid(ln)).astype(BF16), wco[...])
    y_attn = _dot(o, wao[...])
    mix = gate[:, :d_model].astype(F32) * y_conv + gate[:, d_model:].astype(F32) * y_attn
    x1 = x + _rms(_dot(mix.astype(BF16), wo[...]), n_post[...])
    h2 = _rms(x1, n_pre[...]).astype(BF16)
    d_ff = wup.shape[1]
    f = None
    for c0 in range(0, d_ff, ff_chunk):
        a = jnp.square(jnp.maximum(_dot(h2, wup[:, c0:c0 + ff_chunk]), 0.0)).astype(BF16)
        part = _dot(a, wdn[c0:c0 + ff_chunk, :])
        f = part if f is None else f + part
    return x1 + _rms(f, n_mpost[...])


def _post_prompt_kernel(x_ref, uc_ref, o_ref, gate_ref, lng, lnb, wco, wao, wo, wup, wdn,
                        n_post, n_pre, n_mpost, out_ref, *, d_model, ff_chunk):
    out_ref[...] = _mix_and_mlp(x_ref[...], uc_ref[...], o_ref[...], gate_ref[...], lng, lnb,
                                wco, wao, wo, wup, wdn, n_post, n_pre, n_mpost,
                                d_model=d_model, ff_chunk=ff_chunk)


def _post_sample_kernel(x_ref, u_ref, st_ref, o_ref, gate_ref, wdw_ref, bdw_ref, lng, lnb,
                        wco, wao, wo, wup, wdn, n_post, n_pre, n_mpost, out_ref,
                        *, d_model, ff_chunk):
    uc = bdw_ref[...] + u_ref[...] * wdw_ref[CONV_WIDTH - 1:CONV_WIDTH, :]
    for j in range(CONV_WIDTH - 1):
        uc = uc + st_ref[j] * wdw_ref[j:j + 1, :]
    out_ref[...] = _mix_and_mlp(x_ref[...], uc, o_ref[...], gate_ref[...], lng, lnb,
                                wco, wao, wo, wup, wdn, n_post, n_pre, n_mpost,
                                d_model=d_model, ff_chunk=ff_chunk)


def _weight_specs(ws):
    return [_const_spec(w.shape) for w in ws]


def _post_prompt(x, uc, o, gate, weights, *, tm, ff_chunk):
    t, d_model = x.shape
    row = lambda n: pl.BlockSpec((tm, n), lambda i: (i, 0))
    kern = functools.partial(_post_prompt_kernel, d_model=d_model, ff_chunk=ff_chunk)
    return pl.pallas_call(
        kern,
        grid=(t // tm,),
        in_specs=[row(d_model), row(uc.shape[1]), row(o.shape[1]), row(2 * d_model)]
                 + _weight_specs(weights),
        out_specs=row(d_model),
        out_shape=jax.ShapeDtypeStruct((t, d_model), F32),
        compiler_params=_params(1),
    )(x, uc, o, gate, *weights)


def _post_sample(x, u, state_t, o, gate, conv_w, weights, *, ff_chunk):
    t, d_model = x.shape
    full = lambda a: pl.BlockSpec(a.shape, lambda i: (0,) * a.ndim)
    kern = functools.partial(_post_sample_kernel, d_model=d_model, ff_chunk=ff_chunk)
    return pl.pallas_call(
        kern,
        grid=(1,),
        in_specs=[full(x), full(u), full(state_t), full(o), full(gate)]
                 + _weight_specs(conv_w + weights),
        out_specs=pl.BlockSpec((t, d_model), lambda i: (0, 0)),
        out_shape=jax.ShapeDtypeStruct((t, d_model), F32),
        compiler_params=_params(1),
    )(x, u, state_t, o, gate, *conv_w, *weights)


def kernel(x_prompt, x_sample, cache_k, cache_v, state_conv, page_table, norm_mix_pre, norm_mix_post, norm_mlp_pre, norm_mlp_post, w_in, w_dw, b_dw, conv_ln_g, conv_ln_b, w_conv_out, lambda_qk, subln_g, w_attn_out, w_out, w_up, w_down):
    batch, seq, d_model = x_prompt.shape
    dec_batch, dec_seq, _ = x_sample.shape
    assert dec_seq == 1
    depth, _, _, n_heads, head_w = cache_k.shape
    assert head_w == HEAD_WIDTH and state_conv.shape[2] == CONV_WIDTH - 1
    d_conv = state_conv.shape[3]
    attn_w = n_heads * HEAD_WIDTH
    ff_chunk = d_model

    xp = x_prompt.reshape(batch * seq, d_model)
    xs = x_sample.reshape(dec_batch, d_model)
    row = lambda a: a.reshape(1, -1)

    cp, ks, vs, cs = [], [], [], []
    kv_all = None
    for l in range(depth):
        lam_init = 0.8 - 0.6 * math.exp(-0.3 * l)
        w_in_b = w_in[l].astype(BF16)
        conv_w = (w_dw[l], row(b_dw[l]))
        weights = (row(conv_ln_g[l]), row(conv_ln_b[l]),
                   w_conv_out[l].astype(BF16), w_attn_out[l].astype(BF16),
                   w_out[l].astype(BF16), w_up[l].astype(BF16), w_down[l].astype(BF16),
                   row(norm_mix_post[l]), row(norm_mlp_pre[l]), row(norm_mlp_post[l]))
        g_pre = row(norm_mix_pre[l])
        lq = lambda_qk[l]
        sg = row(subln_g[l])

        u, uc, q, k_all, v_all, kb, vb, gate = _inproj_prompt(
            xp, g_pre, w_in_b, *conv_w, kv_all, layer=l, depth=depth, tm=TM_IN, seq=seq,
            d_conv=d_conv, n_heads=n_heads)
        kv_all = (k_all, v_all)
        us, qs, k_s, v_s, gate_s = _inproj_sample(xs, g_pre, w_in_b, d_conv=d_conv,
                                                  attn_w=attn_w)
        seqs = lambda a: a.reshape(batch, seq, attn_w)
        heads = lambda a: a.reshape(dec_batch, n_heads, HEAD_WIDTH)
        o, o_s = _attention(page_table, lq, sg, seqs(q), seqs(kb), seqs(vb), heads(qs),
                            heads(k_s), heads(v_s), cache_k, cache_v, layer=l, tq=TQ,
                            lam_init=lam_init)

        xp = _post_prompt(xp, uc, o.reshape(batch * seq, attn_w), gate, weights, tm=TM_POST,
                          ff_chunk=ff_chunk)
        cp.append(u.reshape(batch, seq, d_conv)[:, seq - (CONV_WIDTH - 1):])

        st = state_conv[l]
        xs = _post_sample(xs, us, st.transpose(1, 0, 2), o_s.reshape(dec_batch, attn_w), gate_s,
                          conv_w, weights, ff_chunk=ff_chunk)
        ks.append(k_s.reshape(dec_batch, 1, n_heads, HEAD_WIDTH))
        vs.append(v_s.reshape(dec_batch, 1, n_heads, HEAD_WIDTH))
        cs.append(jnp.concatenate([st[:, 1:], us[:, None, :]], axis=1))

    k_prompt, v_prompt = (a.reshape(depth, batch, seq, n_heads, HEAD_WIDTH) for a in kv_all)
    return (xp.reshape(batch, seq, d_model), xs.reshape(dec_batch, 1, d_model),
            k_prompt, v_prompt, jnp.stack(cp),
            jnp.stack(ks), jnp.stack(vs), jnp.stack(cs))
```
